```python
import jax
import jax.numpy as jnp
from jax import lax
import numpy as np

D_MODEL = 1024
BATCH = 4
SEQ = 4096
DEPTH = 4
DEC_BATCH = 128
DEC_SEQ = 8
PAST_LEN = 8192
PAGE_SIZE = 128

N_MIXERS = 3
HEAD_DIM = 64
N_HEADS = D_MODEL // HEAD_DIM
FOX_KV_HEADS = 4
FOX_GROUP = N_HEADS // FOX_KV_HEADS
FOX_QBLK = 128
FOX_GATE_BIAS_INIT = 4.0
RWKV_N = 64
RWKV_H = D_MODEL // RWKV_N
RWKV_DECAY_LORA = 64
RWKV_AAA_LORA = 64
RWKV_GATE_LORA = 128
RWKV_GN_EPS = 64e-5
SWA_KV_HEADS = 4
SWA_GROUP = N_HEADS // SWA_KV_HEADS
WINDOW = 128
ROT_DIM = HEAD_DIM // 4
ROPE_THETA = 500000.0
D_FF = 4 * D_MODEL
D_PLE = 256
NORM_EPS = 1e-6
N_FOX = len(range(0, DEPTH, N_MIXERS))
N_RWKV = len(range(1, DEPTH, N_MIXERS))
N_SWA = len(range(2, DEPTH, N_MIXERS))

kernel_name = 'hybrid_fox_rwkv7_swa_decode_step'


def rmsnorm(x, g):
    xf = x.astype(jnp.float32)
    y = xf * lax.rsqrt(jnp.mean(xf * xf, axis=-1, keepdims=True) + NORM_EPS)
    return (y * g.astype(jnp.float32)).astype(x.dtype)


def sq_relu_mlp(h, w_up, w_down):
    return jnp.square(jax.nn.relu(h @ w_up)) @ w_down


def ple_term(x, p, w_proj, w_gate):
    return (p.astype(x.dtype) @ w_proj) * jax.nn.sigmoid(x @ w_gate)


def fox_project(h, w_q, w_k, w_v, w_f, b_f):
    b, t, _ = h.shape
    q = (h @ w_q).reshape(b, t, FOX_KV_HEADS, FOX_GROUP, HEAD_DIM)
    k = (h @ w_k).reshape(b, t, FOX_KV_HEADS, HEAD_DIM)
    v = (h @ w_v).reshape(b, t, FOX_KV_HEADS, HEAD_DIM)
    logf = jax.nn.log_sigmoid((h @ w_f + b_f).astype(jnp.float32))
    return q, k, v, logf


def fox_attend_prompt(q, k, v, logf):
    b, t = q.shape[:2]
    nb = t // FOX_QBLK
    scale = HEAD_DIM ** -0.5
    c = jnp.cumsum(logf.astype(jnp.float32), axis=1)
    c = c.reshape(b, t, FOX_KV_HEADS, FOX_GROUP).transpose(0, 2, 3, 1)
    q_blocks = q.reshape(b, nb, FOX_QBLK, FOX_KV_HEADS, FOX_GROUP, HEAD_DIM).swapaxes(0, 1)
    c_blocks = c.reshape(b, FOX_KV_HEADS, FOX_GROUP, nb, FOX_QBLK).transpose(3, 0, 1, 2, 4)
    q_pos = jnp.arange(t).reshape(nb, FOX_QBLK)
    k_pos = jnp.arange(t)

    def block(args):
        qb, cq, qp = args
        s = jnp.einsum('bqkgd,bskd->bkgqs', qb, k, preferred_element_type=jnp.float32) * scale
        s = s + cq[..., :, None] - c[..., None, :]
        s = jnp.where(k_pos[None, :] <= qp[:, None], s, -jnp.inf)
        p = jax.nn.softmax(s, axis=-1)
        return jnp.einsum('bkgqs,bskd->bqkgd', p.astype(v.dtype), v)

    o = lax.map(block, (q_blocks, c_blocks, q_pos))
    return o.swapaxes(0, 1).reshape(b, t, N_HEADS * HEAD_DIM)


def fox_attend_sample(q, k, v, logf, cache_k, cache_v, cache_logf, page_table):
    db, ds = q.shape[:2]
    past = page_table.shape[1] * PAGE_SIZE
    total = past + ds
    scale = HEAD_DIM ** -0.5
    mask = jnp.arange(total)[None, :] <= (past + jnp.arange(ds))[:, None]

    def one(args):
        pt, qs, ks, vs, lfs = args
        kk = jnp.concatenate([cache_k[pt].reshape(past, FOX_KV_HEADS, HEAD_DIM).astype(ks.dtype), ks], axis=0)
        vv = jnp.concatenate([cache_v[pt].reshape(past, FOX_KV_HEADS, HEAD_DIM).astype(vs.dtype), vs], axis=0)
        lf = jnp.concatenate([cache_logf[pt].reshape(past, N_HEADS).astype(jnp.float32),
                              lfs.astype(jnp.float32)], axis=0)
        c = jnp.cumsum(lf, axis=0).reshape(total, FOX_KV_HEADS, FOX_GROUP).transpose(1, 2, 0)
        s = jnp.einsum('qkgd,skd->kgqs', qs, kk, preferred_element_type=jnp.float32) * scale
        s = s + c[..., past:, None] - c[..., None, :]
        s = jnp.where(mask, s, -jnp.inf)
        p = jax.nn.softmax(s, axis=-1)
        return jnp.einsum('kgqs,skd->qkgd', p.astype(vv.dtype), vv)

    o = lax.map(one, (page_table, q, k, v, logf))
    return o.reshape(db, ds, N_HEADS * HEAD_DIM)


def rwkv_time_mix(h, shift_prev, wkv_prev, mu, w_r, w_k, w_v, w_o, w0, w1, w2,
                  a0, a1, a2, g1, g2, k_k, k_a, r_k, ln_w, ln_b):
    b, t, _ = h.shape
    f32 = jnp.float32
    x_prev = jnp.concatenate([shift_prev[:, None, :].astype(h.dtype), h[:, :-1]], axis=1)
    xx = x_prev - h
    xr = h + xx * mu[0]
    xw = h + xx * mu[1]
    xk = h + xx * mu[2]
    xv = h + xx * mu[3]
    xa = h + xx * mu[4]
    xg = h + xx * mu[5]
    r = xr @ w_r
    k = xk @ w_k
    v = xv @ w_v
    w_log = -jax.nn.softplus(-(w0 + jnp.tanh(xw @ w1) @ w2).astype(f32)) - 0.5
    decay = jnp.exp(-jnp.exp(w_log))
    a = jax.nn.sigmoid((a0 + (xa @ a1) @ a2).astype(f32))
    g = jax.nn.sigmoid(xg @ g1) @ g2

    def heads(z):
        return z.astype(f32).reshape(b, t, RWKV_H, RWKV_N)

    r, v, decay, a = heads(r), heads(v), heads(decay), heads(a)
    kk = heads(k * k_k)
    kk = kk * lax.rsqrt(jnp.maximum(jnp.sum(kk * kk, axis=-1, keepdims=True), 1e-24))
    k = heads(k) * (1.0 + (a - 1.0) * k_a.astype(f32).reshape(RWKV_H, RWKV_N))

    def step(s, inp):
        rt, wt, kt, vt, kkt, at = inp
        sa = jnp.einsum('bhij,bhj->bhi', s, -kkt)
        s = s * wt[:, :, None, :] + sa[..., None] * (kkt * at)[:, :, None, :] + vt[..., None] * kt[:, :, None, :]
        return s, jnp.einsum('bhij,bhj->bhi', s, rt)

    xs = tuple(z.swapaxes(0, 1) for z in (r, decay, k, v, kk, a))
    s_final, y = lax.scan(step, wkv_prev.astype(f32), xs)
    y = y.swapaxes(0, 1)
    mean = jnp.mean(y, axis=-1, keepdims=True)
    var = jnp.mean(jnp.square(y - mean), axis=-1, keepdims=True)
    y = ((y - mean) * lax.rsqrt(var + RWKV_GN_EPS) * ln_w.astype(f32).reshape(RWKV_H, RWKV_N)
         + ln_b.astype(f32).reshape(RWKV_H, RWKV_N))
    y = y + jnp.sum(r * k * r_k.astype(f32), axis=-1, keepdims=True) * v
    out = (y.reshape(b, t, D_MODEL).astype(h.dtype) * g) @ w_o
    return out, s_final.astype(h.dtype), h[:, -1]


def partial_rope(x, pos):
    inv_freq = ROPE_THETA ** (-jnp.arange(0, ROT_DIM, 2, dtype=jnp.float32) / ROT_DIM)
    ang = pos.astype(jnp.float32)[:, None] * inv_freq[None, :]
    cos = jnp.cos(ang)[None, :, None, :]
    sin = jnp.sin(ang)[None, :, None, :]
    half = ROT_DIM // 2
    xr = x[..., :ROT_DIM].astype(jnp.float32)
    x1, x2 = xr[..., :half], xr[..., half:]
    rot = jnp.concatenate([x1 * cos - x2 * sin, x2 * cos + x1 * sin], axis=-1)
    return jnp.concatenate([rot.astype(x.dtype), x[..., ROT_DIM:]], axis=-1)


def swa_project(h, pos, w_q, b_q, w_k, b_k, w_v, b_v):
    b, t, _ = h.shape
    q = partial_rope((h @ w_q + b_q).reshape(b, t, N_HEADS, HEAD_DIM), pos)
    k = partial_rope((h @ w_k + b_k).reshape(b, t, SWA_KV_HEADS, HEAD_DIM), pos)
    v = (h @ w_v + b_v).reshape(b, t, SWA_KV_HEADS, HEAD_DIM)
    return q.reshape(b, t, SWA_KV_HEADS, SWA_GROUP, HEAD_DIM), k, v


def sink_softmax(s, sinks):
    sk = jnp.broadcast_to(sinks.astype(jnp.float32).reshape(SWA_KV_HEADS, SWA_GROUP, 1, 1), s.shape[:-1] + (1,))
    p = jax.nn.softmax(jnp.concatenate([s, sk], axis=-1), axis=-1)
    return p[..., :-1]


def swa_attend_prompt(q, k, v, sinks):
    b, t = q.shape[:2]
    nb = t // WINDOW
    scale = HEAD_DIM ** -0.5
    qb = q.reshape(b, nb, WINDOW, SWA_KV_HEADS, SWA_GROUP, HEAD_DIM)
    kb = k.reshape(b, nb, WINDOW, SWA_KV_HEADS, HEAD_DIM)
    vb = v.reshape(b, nb, WINDOW, SWA_KV_HEADS, HEAD_DIM)
    pad = ((0, 0), (1, 0), (0, 0), (0, 0), (0, 0))
    k_band = jnp.concatenate([jnp.pad(kb, pad)[:, :-1], kb], axis=2)
    v_band = jnp.concatenate([jnp.pad(vb, pad)[:, :-1], vb], axis=2)
    q_pos = jnp.arange(t).reshape(nb, WINDOW)
    k_pos = jnp.arange(nb)[:, None] * WINDOW - WINDOW + jnp.arange(2 * WINDOW)[None, :]
    rel = q_pos[:, :, None] - k_pos[:, None, :]
    mask = (rel >= 0) & (rel < WINDOW) & (k_pos[:, None, :] >= 0)
    s = jnp.einsum('bnqkgd,bnskd->bnkgqs', qb, k_band, preferred_element_type=jnp.float32) * scale
    s = jnp.where(mask[None, :, None, None], s, -jnp.inf)
    p = sink_softmax(s, sinks)
    o = jnp.einsum('bnkgqs,bnskd->bnqkgd', p.astype(v.dtype), v_band).reshape(b, t, N_HEADS * HEAD_DIM)
    keep = min(WINDOW, t)
    return o, k[:, t - keep:], v[:, t - keep:]


def swa_attend_sample(q, k, v, sinks, buf_k, buf_v, past):
    db, ds = q.shape[:2]
    keep = buf_k.shape[1]
    scale = HEAD_DIM ** -0.5
    kk = jnp.concatenate([buf_k.astype(k.dtype), k], axis=1)
    vv = jnp.concatenate([buf_v.astype(v.dtype), v], axis=1)
    k_pos = past - keep + jnp.arange(keep + ds)
    q_pos = past + jnp.arange(ds)
    rel = q_pos[:, None] - k_pos[None, :]
    mask = (rel >= 0) & (rel < WINDOW)
    s = jnp.einsum('bqkgd,bskd->bkgqs', q, kk, preferred_element_type=jnp.float32) * scale
    s = jnp.where(mask, s, -jnp.inf)
    p = sink_softmax(s, sinks)
    o = jnp.einsum('bkgqs,bskd->bqkgd', p.astype(vv.dtype), vv).reshape(db, ds, N_HEADS * HEAD_DIM)
    return o, kk[:, ds:], vv[:, ds:]


def setup_inputs(seed: int = 0) -> dict:
    key = jax.random.key(seed)
    keys = iter(jax.random.split(key, 96))

    def nrm(shape, scale=1.0):
        return jax.random.normal(next(keys), shape, jnp.float32) * scale

    def gain(shape):
        return 1.0 + nrm(shape, 0.05)

    n_pages = PAST_LEN // PAGE_SIZE
    n_used = DEC_BATCH * n_pages
    n_pool = n_used + max(1, n_used // 4)
    w_keep = min(WINDOW, PAST_LEN)
    hd = N_HEADS * HEAD_DIM
    kvd_f = FOX_KV_HEADS * HEAD_DIM
    kvd_s = SWA_KV_HEADS * HEAD_DIM
    fox_kv_shape = (n_pool, PAGE_SIZE, FOX_KV_HEADS, HEAD_DIM)
    fox_lf_shape = (n_pool, PAGE_SIZE, N_HEADS)
    swa_shape = (DEC_BATCH, w_keep, SWA_KV_HEADS, HEAD_DIM)

    x_prompt = nrm((BATCH, SEQ, D_MODEL))
    x_sample = nrm((DEC_BATCH, DEC_SEQ, D_MODEL))
    cache_fox_k_l0 = nrm(fox_kv_shape)
    cache_fox_v_l0 = nrm(fox_kv_shape)
    cache_fox_logf_l0 = jax.nn.log_sigmoid(FOX_GATE_BIAS_INIT + nrm(fox_lf_shape))
    state_rwkv_wkv_l1 = nrm((DEC_BATCH, RWKV_H, RWKV_N, RWKV_N))
    state_rwkv_shift_l1 = nrm((DEC_BATCH, D_MODEL))
    cache_swa_k_l2 = nrm(swa_shape)
    cache_swa_v_l2 = nrm(swa_shape)
    cache_fox_k_l3 = nrm(fox_kv_shape)
    cache_fox_v_l3 = nrm(fox_kv_shape)
    cache_fox_logf_l3 = jax.nn.log_sigmoid(FOX_GATE_BIAS_INIT + nrm(fox_lf_shape))
    page_table = jax.random.permutation(next(keys), n_pool)[:n_used].reshape(DEC_BATCH, n_pages).astype(jnp.int32)
    p_prompt = nrm((DEPTH, BATCH, SEQ, D_PLE))
    p_sample = nrm((DEPTH, DEC_BATCH, DEC_SEQ, D_PLE))

    return {
        'x_prompt': x_prompt,
        'x_sample': x_sample,
        'cache_fox_k_l0': cache_fox_k_l0,
        'cache_fox_v_l0': cache_fox_v_l0,
        'cache_fox_logf_l0': cache_fox_logf_l0,
        'state_rwkv_wkv_l1': state_rwkv_wkv_l1,
        'state_rwkv_shift_l1': state_rwkv_shift_l1,
        'cache_swa_k_l2': cache_swa_k_l2,
        'cache_swa_v_l2': cache_swa_v_l2,
        'cache_fox_k_l3': cache_fox_k_l3,
        'cache_fox_v_l3': cache_fox_v_l3,
        'cache_fox_logf_l3': cache_fox_logf_l3,
        'page_table': page_table,
        'p_prompt': p_prompt,
        'p_sample': p_sample,
        'norm_mix_pre': gain((DEPTH, D_MODEL)),
        'norm_mix_post': gain((DEPTH, D_MODEL)),
        'norm_ffn_pre': gain((DEPTH, D_MODEL)),
        'norm_ffn_post': gain((DEPTH, D_MODEL)),
        'ffn_w_up': nrm((DEPTH, D_MODEL, D_FF), D_MODEL ** -0.5),
        'ffn_w_down': nrm((DEPTH, D_FF, D_MODEL), D_FF ** -0.5),
        'ple_w_proj': nrm((DEPTH, D_PLE, D_MODEL), D_PLE ** -0.5),
        'ple_w_gate': nrm((DEPTH, D_MODEL, D_MODEL), D_MODEL ** -0.5),
        'fox_w_q': nrm((N_FOX, D_MODEL, hd), D_MODEL ** -0.5),
        'fox_w_k': nrm((N_FOX, D_MODEL, kvd_f), D_MODEL ** -0.5),
        'fox_w_v': nrm((N_FOX, D_MODEL, kvd_f), D_MODEL ** -0.5),
        'fox_w_f': nrm((N_FOX, D_MODEL, N_HEADS), D_MODEL ** -0.5),
        'fox_b_f': FOX_GATE_BIAS_INIT + nrm((N_FOX, N_HEADS), 0.1),
        'fox_w_o': nrm((N_FOX, hd, D_MODEL), hd ** -0.5),
        'rwkv_mu': jax.random.uniform(next(keys), (N_RWKV, 6, D_MODEL), jnp.float32),
        'rwkv_w_r': nrm((N_RWKV, D_MODEL, D_MODEL), D_MODEL ** -0.5),
        'rwkv_w_k': nrm((N_RWKV, D_MODEL, D_MODEL), D_MODEL ** -0.5),
        'rwkv_w_v': nrm((N_RWKV, D_MODEL, D_MODEL), D_MODEL ** -0.5),
        'rwkv_w_o': nrm((N_RWKV, D_MODEL, D_MODEL), D_MODEL ** -0.5),
        'rwkv_w0': nrm((N_RWKV, D_MODEL), 0.5),
        'rwkv_w1': nrm((N_RWKV, D_MODEL, RWKV_DECAY_LORA), D_MODEL ** -0.5),
        'rwkv_w2': nrm((N_RWKV, RWKV_DECAY_LORA, D_MODEL), RWKV_DECAY_LORA ** -0.5),
        'rwkv_a0': nrm((N_RWKV, D_MODEL), 0.1),
        'rwkv_a1': nrm((N_RWKV, D_MODEL, RWKV_AAA_LORA), D_MODEL ** -0.5),
        'rwkv_a2': nrm((N_RWKV, RWKV_AAA_LORA, D_MODEL), RWKV_AAA_LORA ** -0.5),
        'rwkv_g1': nrm((N_RWKV, D_MODEL, RWKV_GATE_LORA), D_MODEL ** -0.5),
        'rwkv_g2': nrm((N_RWKV, RWKV_GATE_LORA, D_MODEL), RWKV_GATE_LORA ** -0.5),
        'rwkv_k_k': 0.85 + nrm((N_RWKV, D_MODEL), 0.05),
        'rwkv_k_a': gain((N_RWKV, D_MODEL)),
        'rwkv_r_k': nrm((N_RWKV, RWKV_H, RWKV_N), 0.1),
        'rwkv_ln_w': gain((N_RWKV, D_MODEL)),
        'rwkv_ln_b': nrm((N_RWKV, D_MODEL), 0.02),
        'swa_w_q': nrm((N_SWA, D_MODEL, hd), D_MODEL ** -0.5),
        'swa_b_q': nrm((N_SWA, hd), 0.02),
        'swa_w_k': nrm((N_SWA, D_MODEL, kvd_s), D_MODEL ** -0.5),
        'swa_b_k': nrm((N_SWA, kvd_s), 0.02),
        'swa_w_v': nrm((N_SWA, D_MODEL, kvd_s), D_MODEL ** -0.5),
        'swa_b_v': nrm((N_SWA, kvd_s), 0.02),
        'swa_sinks': nrm((N_SWA, N_HEADS)),
        'swa_w_o': nrm((N_SWA, hd, D_MODEL), hd ** -0.5),
    }


def reference(x_prompt, x_sample, cache_fox_k_l0, cache_fox_v_l0, cache_fox_logf_l0,
              state_rwkv_wkv_l1, state_rwkv_shift_l1, cache_swa_k_l2, cache_swa_v_l2,
              cache_fox_k_l3, cache_fox_v_l3, cache_fox_logf_l3, page_table, p_prompt, p_sample,
              norm_mix_pre, norm_mix_post, norm_ffn_pre, norm_ffn_post, ffn_w_up, ffn_w_down,
              ple_w_proj, ple_w_gate, fox_w_q, fox_w_k, fox_w_v, fox_w_f, fox_b_f, fox_w_o,
              rwkv_mu, rwkv_w_r, rwkv_w_k, rwkv_w_v, rwkv_w_o, rwkv_w0, rwkv_w1, rwkv_w2,
              rwkv_a0, rwkv_a1, rwkv_a2, rwkv_g1, rwkv_g2, rwkv_k_k, rwkv_k_a, rwkv_r_k,
              rwkv_ln_w, rwkv_ln_b, swa_w_q, swa_b_q, swa_w_k, swa_b_k, swa_w_v, swa_b_v,
              swa_sinks, swa_w_o):
    past_len = page_table.shape[1] * PAGE_SIZE
    pos_prompt = jnp.arange(x_prompt.shape[1])
    pos_sample = past_len + jnp.arange(x_sample.shape[1])
    fox_caches = ((cache_fox_k_l0, cache_fox_v_l0, cache_fox_logf_l0),
                  (cache_fox_k_l3, cache_fox_v_l3, cache_fox_logf_l3))
    rwkv_states = ((state_rwkv_wkv_l1, state_rwkv_shift_l1),)
    swa_caches = ((cache_swa_k_l2, cache_swa_v_l2),)

    xp, xs = x_prompt, x_sample
    new_state = []
    for i in range(DEPTH):
        kind, j = i % N_MIXERS, i // N_MIXERS
        hp = rmsnorm(xp, norm_mix_pre[i])
        hs = rmsnorm(xs, norm_mix_pre[i])
        if kind == 0:
            fw = (fox_w_q[j], fox_w_k[j], fox_w_v[j], fox_w_f[j], fox_b_f[j])
            qp, kp, vp, lfp = fox_project(hp, *fw)
            qs, ks, vs, lfs = fox_project(hs, *fw)
            ck, cv, clf = fox_caches[j]
            mp = fox_attend_prompt(qp, kp, vp, lfp) @ fox_w_o[j]
            ms = fox_attend_sample(qs, ks, vs, lfs, ck, cv, clf, page_table) @ fox_w_o[j]
            new_state.append((kp, vp, lfp.astype(hp.dtype), ks, vs, lfs.astype(hs.dtype)))
        elif kind == 1:
            rw = tuple(w[j] for w in (rwkv_mu, rwkv_w_r, rwkv_w_k, rwkv_w_v, rwkv_w_o, rwkv_w0, rwkv_w1,
                                      rwkv_w2, rwkv_a0, rwkv_a1, rwkv_a2, rwkv_g1, rwkv_g2, rwkv_k_k,
                                      rwkv_k_a, rwkv_r_k, rwkv_ln_w, rwkv_ln_b))
            wkv_in, shift_in = rwkv_states[j]
            bp = hp.shape[0]
            mp, wkv_p, shift_p = rwkv_time_mix(hp, jnp.zeros((bp, D_MODEL), hp.dtype),
                                               jnp.zeros((bp, RWKV_H, RWKV_N, RWKV_N), hp.dtype), *rw)
            ms, wkv_s, shift_s = rwkv_time_mix(hs, shift_in, wkv_in, *rw)
            new_state.append((wkv_p, shift_p, wkv_s, shift_s))
        else:
            sw = (swa_w_q[j], swa_b_q[j], swa_w_k[j], swa_b_k[j], swa_w_v[j], swa_b_v[j])
            qp, kp, vp = swa_project(hp, pos_prompt, *sw)
            qs, ks, vs = swa_project(hs, pos_sample, *sw)
            bk, bv = swa_caches[j]
            op, kbp, vbp = swa_attend_prompt(qp, kp, vp, swa_sinks[j])
            os_, kbs, vbs = swa_attend_sample(qs, ks, vs, swa_sinks[j], bk, bv, past_len)
            mp = op @ swa_w_o[j]
            ms = os_ @ swa_w_o[j]
            new_state.append((kbp, vbp, kbs, vbs))
        xp = xp + rmsnorm(mp, norm_mix_post[i])
        xs = xs + rmsnorm(ms, norm_mix_post[i])
        xp = xp + rmsnorm(sq_relu_mlp(rmsnorm(xp, norm_ffn_pre[i]), ffn_w_up[i], ffn_w_down[i]), norm_ffn_post[i])
        xs = xs + rmsnorm(sq_relu_mlp(rmsnorm(xs, norm_ffn_pre[i]), ffn_w_up[i], ffn_w_down[i]), norm_ffn_post[i])
        xp = xp + ple_term(xp, p_prompt[i], ple_w_proj[i], ple_w_gate[i])
        xs = xs + ple_term(xs, p_sample[i], ple_w_proj[i], ple_w_gate[i])

    (fox_k_prompt_l0, fox_v_prompt_l0, fox_logf_prompt_l0,
     fox_k_sample_l0, fox_v_sample_l0, fox_logf_sample_l0) = new_state[0]
    rwkv_wkv_prompt_l1, rwkv_shift_prompt_l1, rwkv_wkv_sample_l1, rwkv_shift_sample_l1 = new_state[1]
    swa_k_prompt_l2, swa_v_prompt_l2, swa_k_sample_l2, swa_v_sample_l2 = new_state[2]
    (fox_k_prompt_l3, fox_v_prompt_l3, fox_logf_prompt_l3,
     fox_k_sample_l3, fox_v_sample_l3, fox_logf_sample_l3) = new_state[3]
    y_prompt, y_sample = xp, xs
    return (y_prompt, y_sample,
            fox_k_prompt_l0, fox_v_prompt_l0, fox_logf_prompt_l0,
            fox_k_sample_l0, fox_v_sample_l0, fox_logf_sample_l0,
            rwkv_wkv_prompt_l1, rwkv_shift_prompt_l1, rwkv_wkv_sample_l1, rwkv_shift_sample_l1,
            swa_k_prompt_l2, swa_v_prompt_l2, swa_k_sample_l2, swa_v_sample_l2,
            fox_k_prompt_l3, fox_v_prompt_l3, fox_logf_prompt_l3,
            fox_k_sample_l3, fox_v_sample_l3, fox_logf_sample_l3)
```

```python
import functools

import numpy as np
import jax
import jax.numpy as jnp
from jax import lax
from jax.experimental import pallas as pl
from jax.experimental.pallas import tpu as pltpu

F32 = jnp.float32
BF16 = jnp.bfloat16

D_MODEL = 1024
HEAD_DIM = 64
N_HEADS = 16
KV_HEADS = 4
D_KV = KV_HEADS * HEAD_DIM
D_FF = 4096
D_PLE = 256
PAGE = 128
WINDOW = 128
ROT_DIM = 16
ROPE_THETA = 500000.0
NORM_EPS = 1e-6
GN_EPS = 64e-5
N_MIXERS = 3
LANES = 128
SUBLANES = 8
NEG = -1e30
VMEM_LIMIT = 48 * 1024 * 1024
SCALE = HEAD_DIM ** -0.5

HEAD_PERM = (0, 4, 1, 5, 2, 6, 3, 7, 8, 12, 9, 13, 10, 14, 11, 15)
Q_COLS = np.concatenate([np.arange(HEAD_DIM) + HEAD_DIM * h for h in HEAD_PERM])
R_COLS = (np.arange(D_MODEL) % N_HEADS) * HEAD_DIM + np.arange(D_MODEL) // N_HEADS


def _params(*sem):
    return pltpu.CompilerParams(dimension_semantics=sem, vmem_limit_bytes=VMEM_LIMIT)


def _dot(a, b):
    return jnp.dot(a, b, preferred_element_type=F32)


def _dot_nt(a, b):
    return lax.dot_general(a, b, (((1,), (1,)), ((), ())), preferred_element_type=F32)


def _rms(x, g):
    return x * lax.rsqrt(jnp.mean(x * x, axis=-1, keepdims=True) + NORM_EPS) * g


def _log_sigmoid(x):
    return jnp.minimum(x, 0.0) - jnp.log1p(jnp.exp(-jnp.abs(x)))


def _softplus(x):
    return jnp.maximum(x, 0.0) + jnp.log1p(jnp.exp(-jnp.abs(x)))


def _split_dot(x, g, parts):
    out = None
    for _ in range(parts):
        piece = x.astype(BF16)
        term = _dot(piece, g)
        out = term if out is None else out + term
        x = x - piece.astype(F32)
    return out


def _tile_sum(x):
    out = x[:, 0:LANES]
    for t in range(1, D_MODEL // LANES):
        out = out + x[:, t * LANES:(t + 1) * LANES]
    return out


def _tile8(x):
    return jnp.concatenate([x] * (D_MODEL // LANES), axis=1)


def _rope(x, a, m, p):
    chunks = []
    for c in range(x.shape[1] // LANES):
        xc = x[:, c * LANES:(c + 1) * LANES]
        chunks.append(xc * a + pltpu.roll(xc, LANES - ROT_DIM // 2, 1) * m + pltpu.roll(xc, ROT_DIM // 2, 1) * p)
    return jnp.concatenate(chunks, axis=1)


def _proj_kernel(x_ref, g_ref, w_ref, b_ref, *rest, rope):
    if rope:
        ra_ref, rm_ref, rp_ref, q_ref, k_ref, v_ref = rest
    else:
        q_ref, k_ref, v_ref, lf_ref = rest
    h = _rms(x_ref[...], g_ref[...]).astype(BF16)
    res = _dot(h, w_ref[...]) + b_ref[...]
    q = res[:, :D_MODEL]
    k = res[:, D_MODEL:D_MODEL + D_KV]
    v = res[:, D_MODEL + D_KV:D_MODEL + 2 * D_KV]
    if rope:
        a, m, p = ra_ref[...], rm_ref[...], rp_ref[...]
        q = _rope(q, a, m, p)
        k = _rope(k, a, m, p)
    else:
        lf_ref[...] = _log_sigmoid(res[:, D_MODEL + 2 * D_KV:])
    q_ref[...] = (q * SCALE).astype(BF16)
    k_ref[...] = k
    v_ref[...] = v


def _proj(x, g, w, b, rope_tabs, tm=512):
    t = x.shape[0]
    n = w.shape[1]
    rope = rope_tabs is not None
    row = lambda i: (i, 0)
    fixed = lambda i: (0, 0)
    in_specs = [pl.BlockSpec((tm, D_MODEL), row), pl.BlockSpec((1, D_MODEL), fixed),
                pl.BlockSpec((D_MODEL, n), fixed), pl.BlockSpec((1, n), fixed)]
    args = [x, g, w, b]
    out_shape = [jax.ShapeDtypeStruct((t, D_MODEL), BF16), jax.ShapeDtypeStruct((t, D_KV), F32),
                 jax.ShapeDtypeStruct((t, D_KV), F32)]
    out_specs = [pl.BlockSpec((tm, D_MODEL), row), pl.BlockSpec((tm, D_KV), row), pl.BlockSpec((tm, D_KV), row)]
    if rope:
        in_specs += [pl.BlockSpec((tm, LANES), row)] * 3
        args += list(rope_tabs)
    else:
        out_shape.append(jax.ShapeDtypeStruct((t, LANES), F32))
        out_specs.append(pl.BlockSpec((tm, LANES), row))
    return pl.pallas_call(
        functools.partial(_proj_kernel, rope=rope), grid=(t // tm,), in_specs=in_specs, out_specs=out_specs,
        out_shape=out_shape, compiler_params=_params("parallel"),
        name="swa_proj" if rope else "fox_proj")(*args)


def _oproj_kernel(o_ref, w_ref, g_ref, x_ref, y_ref):
    m = _dot(o_ref[...], w_ref[...])
    y_ref[...] = x_ref[...] + _rms(m, g_ref[...])


def _oproj(o, w, g, x, tm=512):
    t = x.shape[0]
    row = lambda i: (i, 0)
    fixed = lambda i: (0, 0)
    return pl.pallas_call(
        _oproj_kernel, grid=(t // tm,),
        in_specs=[pl.BlockSpec((tm, D_MODEL), row), pl.BlockSpec((D_MODEL, D_MODEL), fixed),
                  pl.BlockSpec((1, D_MODEL), fixed), pl.BlockSpec((tm, D_MODEL), row)],
        out_specs=pl.BlockSpec((tm, D_MODEL), row), out_shape=jax.ShapeDtypeStruct((t, D_MODEL), F32),
        compiler_params=_params("parallel"), name="out_proj")(o, w, g, x)


def _ffn_kernel(x_ref, gpre_ref, wup_ref, wdn_ref, gpost_ref, p_ref, wp_ref, wg_ref, y_ref, h_s, acc_s):
    kf = pl.program_id(1)

    @pl.when(kf == 0)
    def _():
        h_s[...] = _rms(x_ref[...], gpre_ref[...]).astype(BF16)
        acc_s[...] = jnp.zeros_like(acc_s)

    u = jnp.maximum(_dot(h_s[...], wup_ref[...]), 0.0)
    acc_s[...] += _dot((u * u).astype(BF16), wdn_ref[...])

    @pl.when(kf == pl.num_programs(1) - 1)
    def _():
        x1 = x_ref[...] + _rms(acc_s[...], gpost_ref[...])
        gate = jax.nn.sigmoid(_dot(x1.astype(BF16), wg_ref[...]))
        y_ref[...] = x1 + _dot(p_ref[...].astype(BF16), wp_ref[...]) * gate


def _ffn(x, gpre, wup, wdn, gpost, p, wp, wg, tm=512, tf=512):
    t = x.shape[0]
    row = lambda i, k: (i, 0)
    fixed = lambda i, k: (0, 0)
    return pl.pallas_call(
        _ffn_kernel, grid=(t // tm, D_FF // tf),
        in_specs=[pl.BlockSpec((tm, D_MODEL), row), pl.BlockSpec((1, D_MODEL), fixed),
                  pl.BlockSpec((D_MODEL, tf), lambda i, k: (0, k)), pl.BlockSpec((tf, D_MODEL), lambda i, k: (k, 0)),
                  pl.BlockSpec((1, D_MODEL), fixed), pl.BlockSpec((tm, D_PLE), row),
                  pl.BlockSpec((D_PLE, D_MODEL), fixed), pl.BlockSpec((D_MODEL, D_MODEL), fixed)],
        out_specs=pl.BlockSpec((tm, D_MODEL), row), out_shape=jax.ShapeDtypeStruct((t, D_MODEL), F32),
        scratch_shapes=[pltpu.VMEM((tm, D_MODEL), BF16), pltpu.VMEM((tm, D_MODEL), F32)],
        compiler_params=_params("parallel", "arbitrary"), name="ffn_ple")(x, gpre, wup, wdn, gpost, p, wp, wg)


def _cumsum_kernel(x_ref, c_ref):
    n = x_ref.shape[1]
    lane = lax.broadcasted_iota(jnp.int32, (N_HEADS, LANES), 1)
    carry = jnp.zeros((N_HEADS, 1), F32)
    for c in range(n // LANES):
        blk = x_ref[:, c * LANES:(c + 1) * LANES]
        sh = 1
        while sh < LANES:
            blk = blk + jnp.where(lane >= sh, pltpu.roll(blk, sh, 1), 0.0)
            sh *= 2
        blk = blk + carry
        c_ref[:, c * LANES:(c + 1) * LANES] = blk
        carry = blk[:, LANES - 1:LANES]


def _cumsum(x):
    b, h, t = x.shape
    spec = pl.BlockSpec((None, h, t), lambda i: (i, 0, 0))
    return pl.pallas_call(_cumsum_kernel, grid=(b,), in_specs=[spec], out_specs=spec,
                          out_shape=jax.ShapeDtypeStruct(x.shape, F32), compiler_params=_params("parallel"),
                          name="logf_cumsum")(x)


def _flash_kernel(qi_tab, ki_tab, first_tab, last_tab, q_ref, k_ref, v_ref, *rest, tq, tk, fox):
    if fox:
        cq_ref, ck_ref, o_ref, qm_s, m_s, l_s, acc_s = rest
    else:
        sink_ref, o_ref, qm_s, m_s, l_s, acc_s = rest
    step = pl.program_id(1)
    qi = qi_tab[step]
    ki = ki_tab[step]
    lane = lax.broadcasted_iota(jnp.int32, (tq, LANES), 1)

    @pl.when(first_tab[step] == 1)
    def _():
        m_s[...] = jnp.full_like(m_s, NEG)
        l_s[...] = jnp.zeros_like(l_s)
        acc_s[...] = jnp.zeros_like(acc_s)
        for pp in range(N_HEADS):
            qp = q_ref[:, (pp // 2) * LANES:(pp // 2 + 1) * LANES]
            keep = (lane < HEAD_DIM) if pp % 2 == 0 else (lane >= HEAD_DIM)
            qm_s[pp] = jnp.where(keep, qp, jnp.zeros_like(qp))

    kb = k_ref[...].astype(BF16)
    vb = v_ref[...].astype(BF16)
    row = qi * tq + lax.broadcasted_iota(jnp.int32, (tq, tk), 0)
    col = ki * tk + lax.broadcasted_iota(jnp.int32, (tq, tk), 1)
    mask = col <= row
    if not fox:
        mask = mask & (row - col < WINDOW)
    for pair in range(N_HEADS // 2):
        kv = pair // 4
        kp = kb[:, kv * LANES:(kv + 1) * LANES]
        vp = vb[:, kv * LANES:(kv + 1) * LANES]
        upd = []
        for half in range(2):
            pp = 2 * pair + half
            h = HEAD_PERM[pp]
            s = _dot_nt(qm_s[pp], kp)
            if fox:
                s = s + (cq_ref[:, h:h + 1] - ck_ref[h:h + 1, :])
            s = jnp.where(mask, s, NEG)
            m_old = m_s[pp]
            m_new = jnp.maximum(m_old, jnp.max(s, axis=1, keepdims=True))
            alpha = jnp.exp(m_old - m_new)
            p = jnp.exp(s - m_new)
            l_s[pp] = alpha * l_s[pp] + jnp.sum(p, axis=1, keepdims=True)
            m_s[pp] = m_new
            upd.append((alpha, _dot(p.astype(BF16), vp)))
        a = acc_s[:, pair * LANES:(pair + 1) * LANES]
        acc_s[:, pair * LANES:(pair + 1) * LANES] = jnp.where(
            lane < HEAD_DIM, upd[0][0] * a + upd[0][1], upd[1][0] * a + upd[1][1])

    @pl.when(last_tab[step] == 1)
    def _():
        for pair in range(N_HEADS // 2):
            ls = []
            for half in range(2):
                pp = 2 * pair + half
                l = l_s[pp]
                if not fox:
                    h = HEAD_PERM[pp]
                    l = l + jnp.exp(sink_ref[:, h:h + 1] - m_s[pp])
                ls.append(1.0 / l)
            inv = jnp.where(lane < HEAD_DIM, ls[0], ls[1])
            o_ref[:, pair * LANES:(pair + 1) * LANES] = (acc_s[:, pair * LANES:(pair + 1) * LANES] * inv).astype(BF16)


def _flash(q, k, v, batch, seq, tq, fox, cq=None, ck=None, sinks=None):
    tk = tq
    nq = seq // tq
    pairs = [(a, b) for a in range(nq) for b in range(0 if fox else max(a - 1, 0), a + 1)]
    qi_tab = jnp.asarray([a for a, _ in pairs], jnp.int32)
    ki_tab = jnp.asarray([b for _, b in pairs], jnp.int32)
    first_tab = jnp.asarray([int(b == (0 if fox else max(a - 1, 0))) for a, b in pairs], jnp.int32)
    last_tab = jnp.asarray([int(b == a) for a, b in pairs], jnp.int32)
    qmap = lambda b, s, qt, kt, ft, lt: (b * nq + qt[s], 0)
    kmap = lambda b, s, qt, kt, ft, lt: (b * nq + kt[s], 0)
    in_specs = [pl.BlockSpec((tq, D_MODEL), qmap), pl.BlockSpec((tk, D_KV), kmap), pl.BlockSpec((tk, D_KV), kmap)]
    args = [q, k, v]
    if fox:
        in_specs += [pl.BlockSpec((tq, N_HEADS), qmap),
                     pl.BlockSpec((None, N_HEADS, tk), lambda b, s, qt, kt, ft, lt: (b, 0, kt[s]))]
        args += [cq, ck]
    else:
        in_specs += [pl.BlockSpec((1, N_HEADS), lambda b, s, qt, kt, ft, lt: (0, 0))]
        args += [sinks]
    grid_spec = pltpu.PrefetchScalarGridSpec(
        num_scalar_prefetch=4, grid=(batch, len(pairs)), in_specs=in_specs,
        out_specs=pl.BlockSpec((tq, D_MODEL), qmap),
        scratch_shapes=[pltpu.VMEM((N_HEADS, tq, LANES), BF16), pltpu.VMEM((N_HEADS, tq, 1), F32),
                        pltpu.VMEM((N_HEADS, tq, 1), F32), pltpu.VMEM((tq, D_MODEL), F32)])
    return pl.pallas_call(
        functools.partial(_flash_kernel, tq=tq, tk=tk, fox=fox), grid_spec=grid_spec,
        out_shape=jax.ShapeDtypeStruct((batch * seq, D_MODEL), BF16),
        compiler_params=_params("parallel", "arbitrary"),
        name="fox_prompt_attn" if fox else "swa_prompt_attn")(qi_tab, ki_tab, first_tab, last_tab, *args)


def _expand_heads(x16):
    return jnp.concatenate([jnp.broadcast_to(x16[h:h + 1, :], (SUBLANES, LANES)) for h in range(N_HEADS)], axis=0)


def _decode_update(qbd, k, v, bias, mask, m_s, l_s, acc_s):
    s = _dot_nt(qbd, k.astype(BF16))
    if bias is not None:
        s = s + bias
    if mask is not None:
        s = jnp.where(mask, s, NEG)
    m_old = m_s[...]
    m_new = jnp.maximum(m_old, jnp.max(s, axis=1, keepdims=True))
    alpha = jnp.exp(m_old - m_new)
    p = jnp.exp(s - m_new)
    l_s[...] = alpha * l_s[...] + jnp.sum(p, axis=1, keepdims=True)
    m_s[...] = m_new
    acc_s[...] = alpha * acc_s[...] + _dot(p.astype(BF16), v.astype(BF16))


def _suffix_sums(lf):
    lane = lax.broadcasted_iota(jnp.int32, lf.shape, 1)
    x = lf
    sh = 1
    while sh < LANES:
        x = x + jnp.where(lane < LANES - sh, pltpu.roll(x, LANES - sh, 1), 0.0)
        sh *= 2
    return x


def _fox_sample_kernel(pt_ref, qbd_ref, knew_ref, vnew_ref, lfnew_ref, *rest, pages_per_step):
    n = pages_per_step
    k_refs, v_refs, lf_refs = rest[:n], rest[n:2 * n], rest[2 * n:3 * n]
    o_ref, m_s, l_s, acc_s, carry_s, rowc_s = rest[3 * n:]
    j = pl.program_id(1)
    qbd = qbd_ref[...]
    row_q = lax.broadcasted_iota(jnp.int32, (LANES, LANES), 0) % SUBLANES
    lane = lax.broadcasted_iota(jnp.int32, (LANES, LANES), 1)

    @pl.when(j == 0)
    def _():
        m_s[...] = jnp.full_like(m_s, NEG)
        l_s[...] = jnp.zeros_like(l_s)
        acc_s[...] = jnp.zeros_like(acc_s)
        lf = lfnew_ref[...]
        inc = _suffix_sums(lf)
        bias = _expand_heads(inc - lf)
        rowc = -jnp.sum(jnp.where(lane == row_q, bias, 0.0), axis=1, keepdims=True)
        rowc_s[...] = rowc
        carry_s[...] = inc[:, 0:1]
        _decode_update(qbd, knew_ref[...], vnew_ref[...], bias + rowc, lane <= row_q, m_s, l_s, acc_s)

    for i in range(n):
        lf = lf_refs[i][...]
        inc = _suffix_sums(lf)
        carry = carry_s[...]
        bias = _expand_heads(inc - lf + carry) + rowc_s[...]
        carry_s[...] = carry + inc[:, 0:1]
        _decode_update(qbd, k_refs[i][...], v_refs[i][...], bias, None, m_s, l_s, acc_s)

    @pl.when(j == pl.num_programs(1) - 1)
    def _():
        o_ref[...] = (acc_s[...] * (1.0 / l_s[...])).astype(BF16)


def _fox_sample(page_table, qbd, knew, vnew, lfnew, cache_k, cache_v, cache_lf_t, pages_per_step=8):
    db, n_pages = page_table.shape
    n = pages_per_step
    seq = lambda b, j, pt: (b, 0, 0)

    def page(i):
        return lambda b, j, pt: (pt[b, n_pages - 1 - (n * j + i)], 0, 0)

    in_specs = [pl.BlockSpec((None, LANES, D_KV), seq), pl.BlockSpec((None, PAGE, D_KV), seq),
                pl.BlockSpec((None, PAGE, D_KV), seq), pl.BlockSpec((None, N_HEADS, PAGE), seq)]
    in_specs += [pl.BlockSpec((None, PAGE, D_KV), page(i)) for i in range(n)]
    in_specs += [pl.BlockSpec((None, PAGE, D_KV), page(i)) for i in range(n)]
    in_specs += [pl.BlockSpec((None, N_HEADS, PAGE), page(i)) for i in range(n)]
    grid_spec = pltpu.PrefetchScalarGridSpec(
        num_scalar_prefetch=1, grid=(db, n_pages // n), in_specs=in_specs,
        out_specs=pl.BlockSpec((None, LANES, D_KV), seq),
        scratch_shapes=[pltpu.VMEM((LANES, 1), F32), pltpu.VMEM((LANES, 1), F32), pltpu.VMEM((LANES, D_KV), F32),
                        pltpu.VMEM((N_HEADS, 1), F32), pltpu.VMEM((LANES, 1), F32)])
    return pl.pallas_call(
        functools.partial(_fox_sample_kernel, pages_per_step=n), grid_spec=grid_spec,
        out_shape=jax.ShapeDtypeStruct((db, LANES, D_KV), BF16),
        compiler_params=_params("parallel", "arbitrary"), name="fox_sample_attn")(
            page_table, qbd, knew, vnew, lfnew, *([cache_k] * n), *([cache_v] * n), *([cache_lf_t] * n))


def _swa_sample_kernel(qbd_ref, kbuf_ref, vbuf_ref, knew_ref, vnew_ref, sink_ref, o_ref, m_s, l_s, acc_s):
    qbd = qbd_ref[...]
    row_q = lax.broadcasted_iota(jnp.int32, (LANES, LANES), 0) % SUBLANES
    lane = lax.broadcasted_iota(jnp.int32, (LANES, LANES), 1)
    m_s[...] = jnp.full_like(m_s, NEG)
    l_s[...] = jnp.zeros_like(l_s)
    acc_s[...] = jnp.zeros_like(acc_s)
    _decode_update(qbd, kbuf_ref[...], vbuf_ref[...], None, lane > row_q, m_s, l_s, acc_s)
    _decode_update(qbd, knew_ref[...], vnew_ref[...], None, lane <= row_q, m_s, l_s, acc_s)
    l = l_s[...] + jnp.exp(sink_ref[...] - m_s[...])
    o_ref[...] = (acc_s[...] * (1.0 / l)).astype(BF16)


def _swa_sample(qbd, kbuf, vbuf, knew, vnew, sink_rows):
    db = qbd.shape[0]
    seq = lambda b: (b, 0, 0)
    blk = pl.BlockSpec((None, LANES, D_KV), seq)
    return pl.pallas_call(
        _swa_sample_kernel, grid=(db,),
        in_specs=[blk, blk, blk, blk, blk, pl.BlockSpec((LANES, 1), lambda b: (0, 0))],
        out_specs=blk, out_shape=jax.ShapeDtypeStruct((db, LANES, D_KV), BF16),
        scratch_shapes=[pltpu.VMEM((LANES, 1), F32), pltpu.VMEM((LANES, 1), F32), pltpu.VMEM((LANES, D_KV), F32)],
        compiler_params=_params("parallel"), name="swa_sample_attn")(qbd, kbuf, vbuf, knew, vnew, sink_rows)


def _norm_kernel(x_ref, g_ref, h_ref):
    h_ref[...] = _rms(x_ref[...], g_ref[...])


def _norm(x, g, tm=512):
    t = x.shape[0]
    row = lambda i: (i, 0)
    return pl.pallas_call(
        _norm_kernel, grid=(t // tm,),
        in_specs=[pl.BlockSpec((tm, D_MODEL), row), pl.BlockSpec((1, D_MODEL), lambda i: (0, 0))],
        out_specs=pl.BlockSpec((tm, D_MODEL), row), out_shape=jax.ShapeDtypeStruct((t, D_MODEL), F32),
        compiler_params=_params("parallel"), name="pre_norm")(x, g)


def _rwkv_proj_kernel(h_ref, hp_ref, mu_ref, wr_ref, wk_ref, wv_ref, w1_ref, w2_ref, a1_ref, a2_ref, g1_ref, g2_ref,
                      vec_ref, gmat_ref, r_ref, w_ref, k_ref, v_ref, kk_ref, kka_ref, g_ref):
    h = h_ref[...]
    xx = hp_ref[...] - h

    def mix(i):
        return (h + xx * mu_ref[i:i + 1, :]).astype(BF16)

    w0, a0, k_k, k_a = (vec_ref[i:i + 1, :] for i in range(4))
    r = _dot(mix(0), wr_ref[...])
    k = _dot(mix(2), wk_ref[...])
    v = _dot(mix(3), wv_ref[...])
    lw = _dot(jnp.tanh(_dot(mix(1), w1_ref[...])).astype(BF16), w2_ref[...])
    w_log = -_softplus(-(w0 + lw)) - 0.5
    decay = jnp.exp(-jnp.exp(w_log))
    a = jax.nn.sigmoid(a0 + _dot(_dot(mix(4), a1_ref[...]).astype(BF16), a2_ref[...]))
    g = _dot(jax.nn.sigmoid(_dot(mix(5), g1_ref[...])).astype(BF16), g2_ref[...])
    kk = k * k_k
    ss = _split_dot(_tile_sum(kk * kk), gmat_ref[...], 3)
    kk = kk * _tile8(lax.rsqrt(jnp.maximum(ss, 1e-24)))
    r_ref[...] = r
    w_ref[...] = decay
    k_ref[...] = k * (1.0 + (a - 1.0) * k_a)
    v_ref[...] = v
    kk_ref[...] = kk
    kka_ref[...] = kk * a
    g_ref[...] = g


def _rwkv_proj(h, hp, mu, wr, wk, wv, w1, w2, a1, a2, g1, g2, vecs, gmat, tm=256):
    t = h.shape[0]
    row = lambda i: (i, 0)
    fixed = lambda i: (0, 0)
    full = lambda a: pl.BlockSpec(a.shape, fixed)
    tok = pl.BlockSpec((tm, D_MODEL), row)
    consts = [mu, wr, wk, wv, w1, w2, a1, a2, g1, g2, vecs, gmat]
    return pl.pallas_call(
        _rwkv_proj_kernel, grid=(t // tm,), in_specs=[tok, tok] + [full(a) for a in consts],
        out_specs=[tok] * 7, out_shape=[jax.ShapeDtypeStruct((t, D_MODEL), F32)] * 7,
        compiler_params=_params("parallel"), name="rwkv_proj")(h, hp, *consts)


def _rwkv_scan_kernel(*refs, nb, tc):
    ins = refs[:6 * nb]
    r_refs, w_refs, k_refs, v_refs, kk_refs, kka_refs = (ins[i * nb:(i + 1) * nb] for i in range(6))
    s0_ref, gmat_ref, y_ref, sout_ref, s_s = refs[6 * nb:]
    tt = pl.program_id(1)
    groups = HEAD_DIM // SUBLANES
    eye = (lax.broadcasted_iota(jnp.int32, (SUBLANES, LANES), 1) // N_HEADS
           == lax.broadcasted_iota(jnp.int32, (SUBLANES, LANES), 0))
    gmat = gmat_ref[...]

    @pl.when(tt == 0)
    def _():
        s_s[...] = s0_ref[...]

    def bcast(ref, t):
        return jnp.broadcast_to(ref[pl.ds(t, 1), :], (SUBLANES, D_MODEL))

    def step(t, _):
        kk = [bcast(kk_refs[b], t) for b in range(nb)]
        sa_parts = []
        v_parts = []
        for b in range(nb):
            vrow = bcast(v_refs[b], t)
            for g in range(groups):
                sa_parts.append(_tile_sum(s_s[b, g * SUBLANES:(g + 1) * SUBLANES, :] * kk[b]))
                v_parts.append(jnp.where(eye, vrow[:, g * LANES:(g + 1) * LANES], 0.0))
        sa_all = _split_dot(jnp.concatenate(sa_parts, axis=0), gmat, 2)
        v_all = _split_dot(jnp.concatenate(v_parts, axis=0), gmat, 2)
        y_parts = []
        for b in range(nb):
            w = bcast(w_refs[b], t)
            kka = bcast(kka_refs[b], t)
            k = bcast(k_refs[b], t)
            r = bcast(r_refs[b], t)
            for g in range(groups):
                lo = (b * groups + g) * SUBLANES
                sa = _tile8(-sa_all[lo:lo + SUBLANES, :])
                vv = _tile8(v_all[lo:lo + SUBLANES, :])
                s_new = s_s[b, g * SUBLANES:(g + 1) * SUBLANES, :] * w + sa * kka + vv * k
                s_s[b, g * SUBLANES:(g + 1) * SUBLANES, :] = s_new
                y_parts.append(_tile_sum(s_new * r))
        y_all = _split_dot(jnp.concatenate(y_parts, axis=0), gmat, 2)
        t8 = pl.multiple_of((t // SUBLANES) * SUBLANES, SUBLANES)
        mine = lax.broadcasted_iota(jnp.int32, (SUBLANES, LANES), 0) == t % SUBLANES
        for b in range(nb):
            for g in range(groups):
                lo = (b * groups + g) * SUBLANES
                yrow = jnp.sum(jnp.where(eye, y_all[lo:lo + SUBLANES, :], 0.0), axis=0, keepdims=True)
                cur = y_ref[b, pl.ds(t8, SUBLANES), g * LANES:(g + 1) * LANES]
                y_ref[b, pl.ds(t8, SUBLANES), g * LANES:(g + 1) * LANES] = jnp.where(
                    mine, jnp.broadcast_to(yrow, (SUBLANES, LANES)), cur)
        return 0

    y_ref[...] = jnp.zeros_like(y_ref)
    lax.fori_loop(0, tc, step, 0)

    @pl.when(tt == pl.num_programs(1) - 1)
    def _():
        sout_ref[...] = s_s[...]


def _rwkv_scan(vecs, s0, gmat, batch, seq, row0, nb, tc):
    nt = seq // tc
    blk0 = row0 // tc

    def tok(n):
        return pl.BlockSpec((tc, D_MODEL), lambda bb, tt: (blk0 + (bb * nb + n) * nt + tt, 0))

    in_specs = [tok(n) for _ in range(6) for n in range(nb)]
    args = [a for a in vecs for _ in range(nb)]
    state = pl.BlockSpec((nb, HEAD_DIM, D_MODEL), lambda bb, tt: (bb, 0, 0))
    in_specs += [state, pl.BlockSpec((LANES, LANES), lambda bb, tt: (0, 0))]
    return pl.pallas_call(
        functools.partial(_rwkv_scan_kernel, nb=nb, tc=tc), grid=(batch // nb, nt), in_specs=in_specs,
        out_specs=[pl.BlockSpec((nb, tc, D_MODEL), lambda bb, tt: (bb, tt, 0)), state],
        out_shape=[jax.ShapeDtypeStruct((batch, seq, D_MODEL), F32),
                   jax.ShapeDtypeStruct((batch, HEAD_DIM, D_MODEL), F32)],
        scratch_shapes=[pltpu.VMEM((nb, HEAD_DIM, D_MODEL), F32)],
        compiler_params=_params("parallel", "arbitrary"), name="rwkv_scan")(*args, s0, gmat)


def _rwkv_out_kernel(y_ref, r_ref, k_ref, v_ref, g_ref, x_ref, vec_ref, gmat_ref, wo_ref, gpost_ref, o_ref):
    r_k, ln_w, ln_b = (vec_ref[i:i + 1, :] for i in range(3))
    gmat = gmat_ref[...]
    y = y_ref[...]
    mean = _split_dot(_tile_sum(y), gmat, 3) * (1.0 / HEAD_DIM)
    yc = y - _tile8(mean)
    var = _split_dot(_tile_sum(yc * yc), gmat, 3) * (1.0 / HEAD_DIM)
    yn = yc * _tile8(lax.rsqrt(var + GN_EPS)) * ln_w + ln_b
    bonus = _split_dot(_tile_sum(r_ref[...] * k_ref[...] * r_k), gmat, 3)
    out = (yn + _tile8(bonus) * v_ref[...]) * g_ref[...]
    m = _dot(out.astype(BF16), wo_ref[...])
    o_ref[...] = x_ref[...] + _rms(m, gpost_ref[...])


def _rwkv_out(y, r, k, v, g, x, vecs, gmat, wo, gpost, tm=256):
    t = x.shape[0]
    row = lambda i: (i, 0)
    fixed = lambda i: (0, 0)
    tok = pl.BlockSpec((tm, D_MODEL), row)
    consts = [vecs, gmat, wo, gpost]
    return pl.pallas_call(
        _rwkv_out_kernel, grid=(t // tm,), in_specs=[tok] * 6 + [pl.BlockSpec(a.shape, fixed) for a in consts],
        out_specs=tok, out_shape=jax.ShapeDtypeStruct((t, D_MODEL), F32),
        compiler_params=_params("parallel"), name="rwkv_out")(y, r, k, v, g, x, *consts)


def _block_diag_q(q_perm, db, ds):
    inv = np.argsort(np.asarray(HEAD_PERM))
    q = q_perm.reshape(db, ds, N_HEADS, HEAD_DIM)[:, :, inv]
    q = q.reshape(db, ds, KV_HEADS, N_HEADS // KV_HEADS, HEAD_DIM).transpose(0, 2, 3, 1, 4)
    eye = jnp.eye(KV_HEADS, dtype=q.dtype)
    qbd = q[:, :, :, :, None, :] * eye[None, :, None, None, :, None]
    return qbd.reshape(db, N_HEADS * ds, D_KV)


def _undo_block_diag(o, db, ds):
    o = o.reshape(db, KV_HEADS, N_HEADS // KV_HEADS, ds, KV_HEADS, HEAD_DIM)
    o = jnp.stack([o[:, kv, :, :, kv, :] for kv in range(KV_HEADS)], axis=1)
    o = o.transpose(0, 3, 1, 2, 4).reshape(db, ds, N_HEADS, HEAD_DIM)
    return o[:, :, np.asarray(HEAD_PERM)].reshape(db * ds, D_MODEL)


def _pad_page(x, db, ds):
    return jnp.pad(x.reshape(db, ds, -1), ((0, 0), (0, PAGE - ds), (0, 0)))


def _rope_tables(pos):
    inv_freq = ROPE_THETA ** (-jnp.arange(0, ROT_DIM, 2, dtype=F32) / ROT_DIM)
    ang = pos.astype(F32)[:, None] * inv_freq[None, :]
    cos, sin = jnp.cos(ang), jnp.sin(ang)
    half = ROT_DIM // 2
    ones = jnp.ones((pos.shape[0], HEAD_DIM - ROT_DIM), F32)
    zeros = jnp.zeros((pos.shape[0], HEAD_DIM - ROT_DIM), F32)
    zh = jnp.zeros_like(sin)
    a = jnp.concatenate([cos, cos, ones], axis=1)
    m = jnp.concatenate([-sin, zh, zeros], axis=1)
    p = jnp.concatenate([zh, sin, zeros], axis=1)
    rep = LANES // HEAD_DIM
    return tuple(jnp.tile(z, (1, rep)) for z in (a, m, p))


def kernel(x_prompt, x_sample, cache_fox_k_l0, cache_fox_v_l0, cache_fox_logf_l0, state_rwkv_wkv_l1, state_rwkv_shift_l1, cache_swa_k_l2, cache_swa_v_l2, cache_fox_k_l3, cache_fox_v_l3, cache_fox_logf_l3, page_table, p_prompt, p_sample, norm_mix_pre, norm_mix_post, norm_ffn_pre, norm_ffn_post, ffn_w_up, ffn_w_down, ple_w_proj, ple_w_gate, fox_w_q, fox_w_k, fox_w_v, fox_w_f, fox_b_f, fox_w_o, rwkv_mu, rwkv_w_r, rwkv_w_k, rwkv_w_v, rwkv_w_o, rwkv_w0, rwkv_w1, rwkv_w2, rwkv_a0, rwkv_a1, rwkv_a2, rwkv_g1, rwkv_g2, rwkv_k_k, rwkv_k_a, rwkv_r_k, rwkv_ln_w, rwkv_ln_b, swa_w_q, swa_b_q, swa_w_k, swa_b_k, swa_w_v, swa_b_v, swa_sinks, swa_w_o):
    bp, sp, _ = x_prompt.shape
    db, ds, _ = x_sample.shape
    depth = norm_mix_pre.shape[0]
    tp, ts = bp * sp, db * ds
    n_pages = page_table.shape[1]
    past = n_pages * PAGE
    bf = lambda a: a.astype(BF16)
    vec = lambda a: a.reshape(1, -1).astype(F32)

    x = jnp.concatenate([x_prompt.reshape(tp, D_MODEL), x_sample.reshape(ts, D_MODEL)], axis=0)
    p_all = jnp.concatenate([p_prompt.reshape(depth, tp, D_PLE), p_sample.reshape(depth, ts, D_PLE)], axis=1)
    pos = jnp.concatenate([jnp.tile(jnp.arange(sp), bp), jnp.tile(past + jnp.arange(ds), db)])
    gmat = (np.arange(LANES)[:, None] % N_HEADS == np.arange(LANES)[None, :] % N_HEADS)
    gmat = jnp.asarray(gmat, BF16)
    fox_caches = ((cache_fox_k_l0, cache_fox_v_l0, cache_fox_logf_l0), (cache_fox_k_l3, cache_fox_v_l3, cache_fox_logf_l3))
    new_state = []

    for i in range(depth):
        kind, j = i % N_MIXERS, i // N_MIXERS
        g_pre = vec(norm_mix_pre[i])
        g_post = vec(norm_mix_post[i])
        if kind == 0:
            wf = jnp.pad(fox_w_f[j], ((0, 0), (0, LANES - N_HEADS)))
            w = bf(jnp.concatenate([fox_w_q[j][:, Q_COLS], fox_w_k[j], fox_w_v[j], wf], axis=1))
            b = jnp.concatenate([jnp.zeros((D_MODEL + 2 * D_KV,), F32), fox_b_f[j], jnp.zeros((LANES - N_HEADS,), F32)])
            q, k, v, lf = _proj(x, g_pre, w, vec(b), None)
            lf = lf[:, :N_HEADS]
            lf_p = lf[:tp].reshape(bp, sp, N_HEADS)
            ck = _cumsum(lf_p.transpose(0, 2, 1))
            cq = ck.transpose(0, 2, 1).reshape(tp, N_HEADS)
            o_p = _flash(q, k, v, bp, sp, 512, True, cq=cq, ck=ck)
            ck_cache, cv_cache, clf_cache = fox_caches[j]
            n_pool = ck_cache.shape[0]
            o_s = _fox_sample(
                page_table, _block_diag_q(q[tp:], db, ds), _pad_page(k[tp:], db, ds), _pad_page(v[tp:], db, ds),
                _pad_page(lf[tp:], db, ds).transpose(0, 2, 1), ck_cache.reshape(n_pool, PAGE, D_KV),
                cv_cache.reshape(n_pool, PAGE, D_KV), clf_cache.transpose(0, 2, 1))
            o = jnp.concatenate([o_p, _undo_block_diag(o_s, db, ds)], axis=0)
            x = _oproj(o, bf(fox_w_o[j][Q_COLS, :]), g_post, x)
            new_state.append((k[:tp].reshape(bp, sp, KV_HEADS, HEAD_DIM), v[:tp].reshape(bp, sp, KV_HEADS, HEAD_DIM),
                              lf_p, k[tp:].reshape(db, ds, KV_HEADS, HEAD_DIM),
                              v[tp:].reshape(db, ds, KV_HEADS, HEAD_DIM), lf[tp:].reshape(db, ds, N_HEADS)))
        elif kind == 1:
            h = _norm(x, g_pre)
            h_p = h[:tp].reshape(bp, sp, D_MODEL)
            h_s = h[tp:].reshape(db, ds, D_MODEL)
            hp = jnp.concatenate([
                jnp.concatenate([jnp.zeros((bp, 1, D_MODEL), F32), h_p[:, :-1]], axis=1).reshape(tp, D_MODEL),
                jnp.concatenate([state_rwkv_shift_l1[:, None, :], h_s[:, :-1]], axis=1).reshape(ts, D_MODEL)], axis=0)
            pc = lambda a: a[:, R_COLS]
            vecs = jnp.stack([rwkv_w0[j][R_COLS], rwkv_a0[j][R_COLS], rwkv_k_k[j][R_COLS], rwkv_k_a[j][R_COLS]])
            r, w, k, v, kk, kka, g = _rwkv_proj(
                h, hp, rwkv_mu[j], bf(pc(rwkv_w_r[j])), bf(pc(rwkv_w_k[j])), bf(pc(rwkv_w_v[j])), bf(rwkv_w1[j]),
                bf(pc(rwkv_w2[j])), bf(rwkv_a1[j]), bf(pc(rwkv_a2[j])), bf(rwkv_g1[j]), bf(pc(rwkv_g2[j])), vecs, gmat)
            scan_in = (r, w, k, v, kk, kka)
            to_lanes = lambda s: s.transpose(0, 2, 3, 1).reshape(s.shape[0], HEAD_DIM, D_MODEL)
            from_lanes = lambda s: s.reshape(s.shape[0], HEAD_DIM, HEAD_DIM, N_HEADS).transpose(0, 3, 1, 2)
            y_p, s_p = _rwkv_scan(scan_in, jnp.zeros((bp, HEAD_DIM, D_MODEL), F32), gmat, bp, sp, 0, bp, 64)
            y_s, s_s = _rwkv_scan(scan_in, to_lanes(state_rwkv_wkv_l1.astype(F32)), gmat, db, ds, tp, 4, ds)
            y = jnp.concatenate([y_p.reshape(tp, D_MODEL), y_s.reshape(ts, D_MODEL)], axis=0)
            ovecs = jnp.stack([rwkv_r_k[j].reshape(-1)[R_COLS], rwkv_ln_w[j][R_COLS], rwkv_ln_b[j][R_COLS]])
            x = _rwkv_out(y, r, k, v, g, x, ovecs, gmat, bf(rwkv_w_o[j][R_COLS, :]), g_post)
            new_state.append((from_lanes(s_p), h_p[:, -1], from_lanes(s_s), h_s[:, -1]))
        else:
            w = bf(jnp.concatenate([swa_w_q[j][:, Q_COLS], swa_w_k[j], swa_w_v[j]], axis=1))
            b = jnp.concatenate([swa_b_q[j][Q_COLS], swa_b_k[j], swa_b_v[j]])
            q, k, v = _proj(x, g_pre, w, vec(b), _rope_tables(pos))
            o_p = _flash(q, k, v, bp, sp, WINDOW, False, sinks=vec(swa_sinks[j]))
            keep = cache_swa_k_l2.shape[1]
            k_s = k[tp:].reshape(db, ds, D_KV)
            v_s = v[tp:].reshape(db, ds, D_KV)
            kbuf = cache_swa_k_l2.reshape(db, keep, D_KV)
            vbuf = cache_swa_v_l2.reshape(db, keep, D_KV)
            sink_rows = jnp.repeat(swa_sinks[j].astype(F32), ds).reshape(N_HEADS * ds, 1)
            o_s = _swa_sample(_block_diag_q(q[tp:], db, ds), kbuf, vbuf, _pad_page(k[tp:], db, ds),
                              _pad_page(v[tp:], db, ds), sink_rows)
            o = jnp.concatenate([o_p, _undo_block_diag(o_s, db, ds)], axis=0)
            x = _oproj(o, bf(swa_w_o[j][Q_COLS, :]), g_post, x)
            k_p = k[:tp].reshape(bp, sp, KV_HEADS, HEAD_DIM)
            v_p = v[:tp].reshape(bp, sp, KV_HEADS, HEAD_DIM)
            wk = min(WINDOW, sp)
            new_state.append((k_p[:, sp - wk:], v_p[:, sp - wk:],
                              jnp.concatenate([kbuf, k_s], axis=1)[:, ds:].reshape(db, keep, KV_HEADS, HEAD_DIM),
                              jnp.concatenate([vbuf, v_s], axis=1)[:, ds:].reshape(db, keep, KV_HEADS, HEAD_DIM)))
        x = _ffn(x, vec(norm_ffn_pre[i]), bf(ffn_w_up[i]), bf(ffn_w_down[i]), vec(norm_ffn_post[i]), p_all[i],
                 bf(ple_w_proj[i]), bf(ple_w_gate[i]))

    outs = [x[:tp].reshape(bp, sp, D_MODEL), x[tp:].reshape(db, ds, D_MODEL)]
    for st in new_state:
        outs.extend(st)
    return tuple(outs)
```

```python
import functools

import numpy as np
import jax
import jax.numpy as jnp
from jax import lax
from jax.experimental import pallas as pl
from jax.experimental.pallas import tpu as pltpu

F32 = jnp.float32
BF16 = jnp.bfloat16

D_MODEL = 1024
HEAD_DIM = 64
N_HEADS = 16
KV_HEADS = 4
D_KV = KV_HEADS * HEAD_DIM
D_FF = 4096
D_PLE = 256
PAGE = 128
WINDOW = 128
ROT_DIM = 16
ROPE_THETA = 500000.0
NORM_EPS = 1e-6
GN_EPS = 64e-5
N_MIXERS = 3
LANES = 128
SUBLANES = 8
NEG = -1e30
VMEM_LIMIT = 48 * 1024 * 1024
SCALE = HEAD_DIM ** -0.5

HEAD_PERM = (0, 4, 1, 5, 2, 6, 3, 7, 8, 12, 9, 13, 10, 14, 11, 15)
Q_COLS = np.concatenate([np.arange(HEAD_DIM) + HEAD_DIM * h for h in HEAD_PERM])
R_COLS = (np.arange(D_MODEL) % N_HEADS) * HEAD_DIM + np.arange(D_MODEL) // N_HEADS


def _params(*sem):
    return pltpu.CompilerParams(dimension_semantics=sem, vmem_limit_bytes=VMEM_LIMIT)


def _dot(a, b):
    return jnp.dot(a, b, preferred_element_type=F32)


def _dot_nt(a, b):
    return lax.dot_general(a, b, (((1,), (1,)), ((), ())), preferred_element_type=F32)


def _rms(x, g):
    return x * lax.rsqrt(jnp.mean(x * x, axis=-1, keepdims=True) + NORM_EPS) * g


def _log_sigmoid(x):
    return jnp.minimum(x, 0.0) - jnp.log1p(jnp.exp(-jnp.abs(x)))


def _softplus(x):
    return jnp.maximum(x, 0.0) + jnp.log1p(jnp.exp(-jnp.abs(x)))


def _split_dot(x, g, parts):
    out = None
    for _ in range(parts):
        piece = x.astype(BF16)
        term = _dot(piece, g)
        out = term if out is None else out + term
        x = x - piece.astype(F32)
    return out


def _tile_sum(x):
    out = x[:, 0:LANES]
    for t in range(1, D_MODEL // LANES):
        out = out + x[:, t * LANES:(t + 1) * LANES]
    return out


def _tile8(x):
    return jnp.concatenate([x] * (D_MODEL // LANES), axis=1)


def _rope(x, a, m, p):
    chunks = []
    for c in range(x.shape[1] // LANES):
        xc = x[:, c * LANES:(c + 1) * LANES]
        chunks.append(xc * a + pltpu.roll(xc, LANES - ROT_DIM // 2, 1) * m + pltpu.roll(xc, ROT_DIM // 2, 1) * p)
    return jnp.concatenate(chunks, axis=1)


def _proj_kernel(x_ref, g_ref, w_ref, b_ref, *rest, rope):
    if rope:
        ra_ref, rm_ref, rp_ref, q_ref, k_ref, v_ref = rest
    else:
        q_ref, k_ref, v_ref, lf_ref = rest
    h = _rms(x_ref[...], g_ref[...]).astype(BF16)
    res = _dot(h, w_ref[...]) + b_ref[...]
    q = res[:, :D_MODEL]
    k = res[:, D_MODEL:D_MODEL + D_KV]
    v = res[:, D_MODEL + D_KV:D_MODEL + 2 * D_KV]
    if rope:
        a, m, p = ra_ref[...], rm_ref[...], rp_ref[...]
        q = _rope(q, a, m, p)
        k = _rope(k, a, m, p)
    else:
        lf_ref[...] = _log_sigmoid(res[:, D_MODEL + 2 * D_KV:])
    q_ref[...] = (q * SCALE).astype(BF16)
    k_ref[...] = k
    v_ref[...] = v


def _proj(x, g, w, b, rope_tabs, tm=512):
    t = x.shape[0]
    n = w.shape[1]
    rope = rope_tabs is not None
    row = lambda i: (i, 0)
    fixed = lambda i: (0, 0)
    in_specs = [pl.BlockSpec((tm, D_MODEL), row), pl.BlockSpec((1, D_MODEL), fixed),
                pl.BlockSpec((D_MODEL, n), fixed), pl.BlockSpec((1, n), fixed)]
    args = [x, g, w, b]
    out_shape = [jax.ShapeDtypeStruct((t, D_MODEL), BF16), jax.ShapeDtypeStruct((t, D_KV), F32),
                 jax.ShapeDtypeStruct((t, D_KV), F32)]
    out_specs = [pl.BlockSpec((tm, D_MODEL), row), pl.BlockSpec((tm, D_KV), row), pl.BlockSpec((tm, D_KV), row)]
    if rope:
        in_specs += [pl.BlockSpec((tm, LANES), row)] * 3
        args += list(rope_tabs)
    else:
        out_shape.append(jax.ShapeDtypeStruct((t, LANES), F32))
        out_specs.append(pl.BlockSpec((tm, LANES), row))
    return pl.pallas_call(
        functools.partial(_proj_kernel, rope=rope), grid=(t // tm,), in_specs=in_specs, out_specs=out_specs,
        out_shape=out_shape, compiler_params=_params("parallel"),
        name="swa_proj" if rope else "fox_proj")(*args)


def _oproj_kernel(o_ref, w_ref, g_ref, x_ref, y_ref):
    m = _dot(o_ref[...], w_ref[...])
    y_ref[...] = x_ref[...] + _rms(m, g_ref[...])


def _oproj(o, w, g, x, tm=512):
    t = x.shape[0]
    row = lambda i: (i, 0)
    fixed = lambda i: (0, 0)
    return pl.pallas_call(
        _oproj_kernel, grid=(t // tm,),
        in_specs=[pl.BlockSpec((tm, D_MODEL), row), pl.BlockSpec((D_MODEL, D_MODEL), fixed),
                  pl.BlockSpec((1, D_MODEL), fixed), pl.BlockSpec((tm, D_MODEL), row)],
        out_specs=pl.BlockSpec((tm, D_MODEL), row), out_shape=jax.ShapeDtypeStruct((t, D_MODEL), F32),
        compiler_params=_params("parallel"), name="out_proj")(o, w, g, x)


def _ffn_kernel(x_ref, gpre_ref, wup_ref, wdn_ref, gpost_ref, p_ref, wp_ref, wg_ref, y_ref, h_s, acc_s):
    kf = pl.program_id(1)

    @pl.when(kf == 0)
    def _():
        h_s[...] = _rms(x_ref[...], gpre_ref[...]).astype(BF16)
        acc_s[...] = jnp.zeros_like(acc_s)

    u = jnp.maximum(_dot(h_s[...], wup_ref[...]), 0.0)
    acc_s[...] += _dot((u * u).astype(BF16), wdn_ref[...])

    @pl.when(kf == pl.num_programs(1) - 1)
    def _():
        x1 = x_ref[...] + _rms(acc_s[...], gpost_ref[...])
        gate = jax.nn.sigmoid(_dot(x1.astype(BF16), wg_ref[...]))
        y_ref[...] = x1 + _dot(p_ref[...].astype(BF16), wp_ref[...]) * gate


def _ffn(x, gpre, wup, wdn, gpost, p, wp, wg, tm=512, tf=512):
    t = x.shape[0]
    row = lambda i, k: (i, 0)
    fixed = lambda i, k: (0, 0)
    return pl.pallas_call(
        _ffn_kernel, grid=(t // tm, D_FF // tf),
        in_specs=[pl.BlockSpec((tm, D_MODEL), row), pl.BlockSpec((1, D_MODEL), fixed),
                  pl.BlockSpec((D_MODEL, tf), lambda i, k: (0, k)), pl.BlockSpec((tf, D_MODEL), lambda i, k: (k, 0)),
                  pl.BlockSpec((1, D_MODEL), fixed), pl.BlockSpec((tm, D_PLE), row),
                  pl.BlockSpec((D_PLE, D_MODEL), fixed), pl.BlockSpec((D_MODEL, D_MODEL), fixed)],
        out_specs=pl.BlockSpec((tm, D_MODEL), row), out_shape=jax.ShapeDtypeStruct((t, D_MODEL), F32),
        scratch_shapes=[pltpu.VMEM((tm, D_MODEL), BF16), pltpu.VMEM((tm, D_MODEL), F32)],
        compiler_params=_params("parallel", "arbitrary"), name="ffn_ple")(x, gpre, wup, wdn, gpost, p, wp, wg)


def _cumsum_kernel(x_ref, c_ref):
    n = x_ref.shape[1]
    lane = lax.broadcasted_iota(jnp.int32, (N_HEADS, LANES), 1)
    carry = jnp.zeros((N_HEADS, 1), F32)
    for c in range(n // LANES):
        blk = x_ref[:, c * LANES:(c + 1) * LANES]
        sh = 1
        while sh < LANES:
            blk = blk + jnp.where(lane >= sh, pltpu.roll(blk, sh, 1), 0.0)
            sh *= 2
        blk = blk + carry
        c_ref[:, c * LANES:(c + 1) * LANES] = blk
        carry = blk[:, LANES - 1:LANES]


def _cumsum(x):
    b, h, t = x.shape
    spec = pl.BlockSpec((None, h, t), lambda i: (i, 0, 0))
    return pl.pallas_call(_cumsum_kernel, grid=(b,), in_specs=[spec], out_specs=spec,
                          out_shape=jax.ShapeDtypeStruct(x.shape, F32), compiler_params=_params("parallel"),
                          name="logf_cumsum")(x)


def _flash_kernel(qi_tab, ki_tab, first_tab, last_tab, q_ref, k_ref, v_ref, cq_ref, ck_ref, o_ref, qm_s, m_s, l_s, acc_s,
                  *, tq, tk):
    step = pl.program_id(1)
    qi = qi_tab[step]
    ki = ki_tab[step]
    lane = lax.broadcasted_iota(jnp.int32, (tq, LANES), 1)
    rep = tk // LANES

    @pl.when(first_tab[step] == 1)
    def _():
        m_s[...] = jnp.full_like(m_s, NEG)
        l_s[...] = jnp.zeros_like(l_s)
        acc_s[...] = jnp.zeros_like(acc_s)
        for pp in range(N_HEADS):
            qp = q_ref[:, (pp // 2) * LANES:(pp // 2 + 1) * LANES]
            keep = (lane < HEAD_DIM) if pp % 2 == 0 else (lane >= HEAD_DIM)
            qm_s[pp] = jnp.where(keep, qp, jnp.zeros_like(qp))

    kb = k_ref[...].astype(BF16)
    vb = v_ref[...].astype(BF16)
    row = qi * tq + lax.broadcasted_iota(jnp.int32, (tq, tk), 0)
    col = ki * tk + lax.broadcasted_iota(jnp.int32, (tq, tk), 1)
    mask = col <= row
    for pair in range(N_HEADS // 2):
        kv = pair // 4
        kp = kb[:, kv * LANES:(kv + 1) * LANES]
        vp = vb[:, kv * LANES:(kv + 1) * LANES]
        upd = []
        for half in range(2):
            pp = 2 * pair + half
            h = HEAD_PERM[pp]
            s = _dot_nt(qm_s[pp], kp) + (cq_ref[:, h:h + 1] - ck_ref[h:h + 1, :])
            s = jnp.where(mask, s, NEG)
            m_old = m_s[pp]
            m_new = jnp.maximum(m_old, jnp.max(s, axis=1, keepdims=True))
            alpha = jnp.exp(m_old - m_new)
            p = jnp.exp(s - jnp.concatenate([m_new] * rep, axis=1))
            l_s[pp] = alpha * l_s[pp] + jnp.sum(p, axis=1, keepdims=True)
            m_s[pp] = m_new
            upd.append((alpha, _dot(p.astype(BF16), vp)))
        a = acc_s[:, pair * LANES:(pair + 1) * LANES]
        acc_s[:, pair * LANES:(pair + 1) * LANES] = jnp.where(
            lane < HEAD_DIM, upd[0][0] * a + upd[0][1], upd[1][0] * a + upd[1][1])

    @pl.when(last_tab[step] == 1)
    def _():
        for pair in range(N_HEADS // 2):
            inv = jnp.where(lane < HEAD_DIM, 1.0 / l_s[2 * pair], 1.0 / l_s[2 * pair + 1])
            o_ref[:, pair * LANES:(pair + 1) * LANES] = (acc_s[:, pair * LANES:(pair + 1) * LANES] * inv).astype(BF16)


def _flash(q, k, v, cq, ck, batch, seq, tq=512):
    tk = tq
    nq = seq // tq
    pairs = [(a, b) for a in range(nq) for b in range(a + 1)]
    qi_tab = jnp.asarray([a for a, _ in pairs], jnp.int32)
    ki_tab = jnp.asarray([b for _, b in pairs], jnp.int32)
    first_tab = jnp.asarray([int(b == 0) for a, b in pairs], jnp.int32)
    last_tab = jnp.asarray([int(b == a) for a, b in pairs], jnp.int32)
    qmap = lambda b, s, qt, kt, ft, lt: (b * nq + qt[s], 0)
    kmap = lambda b, s, qt, kt, ft, lt: (b * nq + kt[s], 0)
    in_specs = [pl.BlockSpec((tq, D_MODEL), qmap), pl.BlockSpec((tk, D_KV), kmap), pl.BlockSpec((tk, D_KV), kmap),
                pl.BlockSpec((tq, N_HEADS), qmap),
                pl.BlockSpec((None, N_HEADS, tk), lambda b, s, qt, kt, ft, lt: (b, 0, kt[s]))]
    grid_spec = pltpu.PrefetchScalarGridSpec(
        num_scalar_prefetch=4, grid=(batch, len(pairs)), in_specs=in_specs,
        out_specs=pl.BlockSpec((tq, D_MODEL), qmap),
        scratch_shapes=[pltpu.VMEM((N_HEADS, tq, LANES), BF16), pltpu.VMEM((N_HEADS, tq, LANES), F32),
                        pltpu.VMEM((N_HEADS, tq, LANES), F32), pltpu.VMEM((tq, D_MODEL), F32)])
    return pl.pallas_call(
        functools.partial(_flash_kernel, tq=tq, tk=tk), grid_spec=grid_spec,
        out_shape=jax.ShapeDtypeStruct((batch * seq, D_MODEL), BF16),
        compiler_params=_params("parallel", "arbitrary"),
        name="fox_prompt_attn")(qi_tab, ki_tab, first_tab, last_tab, q, k, v, cq, ck)


def _swa_band_kernel(q_ref, kp_ref, ko_ref, vp_ref, vo_ref, sink_ref, o_ref, *, tq):
    qi = pl.program_id(1)
    nk = WINDOW + tq
    kb = jnp.concatenate([kp_ref[...], ko_ref[...]], axis=0).astype(BF16)
    vb = jnp.concatenate([vp_ref[...], vo_ref[...]], axis=0).astype(BF16)
    lane = lax.broadcasted_iota(jnp.int32, (tq, LANES), 1)
    a = lax.broadcasted_iota(jnp.int32, (tq, nk), 0)
    c = lax.broadcasted_iota(jnp.int32, (tq, nk), 1)
    mask = (c > a) & (c <= a + WINDOW) & (qi * tq + c >= WINDOW)
    for pair in range(N_HEADS // 2):
        kv = pair // 4
        kp = kb[:, kv * LANES:(kv + 1) * LANES]
        vp = vb[:, kv * LANES:(kv + 1) * LANES]
        qp = q_ref[:, pair * LANES:(pair + 1) * LANES]
        outs = []
        for half in range(2):
            h = HEAD_PERM[2 * pair + half]
            keep = (lane < HEAD_DIM) if half == 0 else (lane >= HEAD_DIM)
            s = _dot_nt(jnp.where(keep, qp, jnp.zeros_like(qp)), kp)
            s = jnp.where(mask, s, NEG)
            sink = sink_ref[:, h:h + 1]
            m = jnp.maximum(jnp.max(s, axis=1, keepdims=True), sink)
            p = jnp.exp(s - m)
            l = jnp.sum(p, axis=1, keepdims=True) + jnp.exp(sink - m)
            outs.append(_dot(p.astype(BF16), vp) * (1.0 / l))
        o_ref[:, pair * LANES:(pair + 1) * LANES] = jnp.where(lane < HEAD_DIM, outs[0], outs[1]).astype(BF16)


def _swa_band(q, k, v, sinks, batch, seq, tq=256):
    nq = seq // tq
    r = tq // WINDOW
    qmap = lambda b, i: (b * nq + i, 0)
    pmap = lambda b, i: (jnp.maximum((b * nq + i) * r - 1, 0), 0)
    return pl.pallas_call(
        functools.partial(_swa_band_kernel, tq=tq), grid=(batch, nq),
        in_specs=[pl.BlockSpec((tq, D_MODEL), qmap), pl.BlockSpec((WINDOW, D_KV), pmap), pl.BlockSpec((tq, D_KV), qmap),
                  pl.BlockSpec((WINDOW, D_KV), pmap), pl.BlockSpec((tq, D_KV), qmap),
                  pl.BlockSpec((1, N_HEADS), lambda b, i: (0, 0))],
        out_specs=pl.BlockSpec((tq, D_MODEL), qmap), out_shape=jax.ShapeDtypeStruct((batch * seq, D_MODEL), BF16),
        compiler_params=_params("parallel", "parallel"), name="swa_prompt_attn")(q, k, k, v, v, sinks)


def _expand_heads(x16):
    return jnp.concatenate([jnp.broadcast_to(x16[h:h + 1, :], (SUBLANES, LANES)) for h in range(N_HEADS)], axis=0)


def _decode_update(qbd, k, v, bias, mask, m_s, l_s, acc_s):
    s = _dot_nt(qbd, k.astype(BF16))
    if bias is not None:
        s = s + bias
    if mask is not None:
        s = jnp.where(mask, s, NEG)
    m_old = m_s[...]
    m_new = jnp.maximum(m_old, jnp.max(s, axis=1, keepdims=True))
    alpha = jnp.exp(m_old - m_new)
    p = jnp.exp(s - m_new)
    l_s[...] = alpha * l_s[...] + jnp.sum(p, axis=1, keepdims=True)
    m_s[...] = m_new
    acc_s[...] = alpha * acc_s[...] + _dot(p.astype(BF16), v.astype(BF16))


def _suffix_sums(lf):
    lane = lax.broadcasted_iota(jnp.int32, lf.shape, 1)
    x = lf
    sh = 1
    while sh < LANES:
        x = x + jnp.where(lane < LANES - sh, pltpu.roll(x, LANES - sh, 1), 0.0)
        sh *= 2
    return x


def _fox_sample_kernel(pt_ref, q_ref, knew_ref, vnew_ref, lfnew_ref, *rest, pages_per_step):
    n = pages_per_step
    k_refs, v_refs, lf_refs = rest[:n], rest[n:2 * n], rest[2 * n:3 * n]
    o_ref, m_s, l_s, acc_s, carry_s, rowc_s = rest[3 * n:]
    j = pl.program_id(1)
    rows = N_HEADS * SUBLANES // KV_HEADS
    row_q = lax.broadcasted_iota(jnp.int32, (LANES, LANES), 0) % SUBLANES
    lane = lax.broadcasted_iota(jnp.int32, (LANES, LANES), 1)

    def scores(ks):
        return jnp.concatenate([_dot_nt(q_ref[kv], ks[kv]) for kv in range(KV_HEADS)], axis=0)

    def update(s, vs):
        m_old = m_s[...]
        m_new = jnp.maximum(m_old, jnp.max(s, axis=1, keepdims=True))
        alpha = jnp.exp(m_old - m_new)
        p = jnp.exp(s - m_new)
        l_s[...] = alpha * l_s[...] + jnp.sum(p, axis=1, keepdims=True)
        m_s[...] = m_new
        pb = p.astype(BF16)
        pv = jnp.concatenate([_dot(pb[kv * rows:(kv + 1) * rows, :], vs[kv]) for kv in range(KV_HEADS)], axis=0)
        acc_s[...] = alpha * acc_s[...] + pv

    @pl.when(j == 0)
    def _():
        m_s[...] = jnp.full_like(m_s, NEG)
        l_s[...] = jnp.zeros_like(l_s)
        acc_s[...] = jnp.zeros_like(acc_s)
        lf = lfnew_ref[...]
        inc = _suffix_sums(lf)
        bias = _expand_heads(inc - lf)
        rowc = -jnp.sum(jnp.where(lane == row_q, bias, 0.0), axis=1, keepdims=True)
        rowc_s[...] = rowc
        carry_s[...] = inc[:, 0:1]
        s = scores([knew_ref[kv].astype(BF16) for kv in range(KV_HEADS)]) + (bias + rowc)
        update(jnp.where(lane <= row_q, s, NEG), [vnew_ref[kv].astype(BF16) for kv in range(KV_HEADS)])

    def gather(refs, kv):
        return jnp.concatenate([r[pl.ds(kv, PAGE, stride=KV_HEADS), :] for r in refs], axis=0).astype(BF16)

    carry = carry_s[...]
    biases = []
    for i in range(n):
        lf = lf_refs[i][...]
        inc = _suffix_sums(lf)
        biases.append(_expand_heads(inc - lf + carry))
        carry = carry + inc[:, 0:1]
    carry_s[...] = carry
    s = scores([gather(k_refs, kv) for kv in range(KV_HEADS)])
    update(s + (jnp.concatenate(biases, axis=1) + rowc_s[...]), [gather(v_refs, kv) for kv in range(KV_HEADS)])

    @pl.when(j == pl.num_programs(1) - 1)
    def _():
        o_ref[...] = (acc_s[...] * (1.0 / l_s[...])).astype(BF16)


def _fox_sample(page_table, q4, knew4, vnew4, lfnew, cache_k, cache_v, cache_lf_t, pages_per_step=16):
    db, n_pages = page_table.shape
    n = pages_per_step
    rows = N_HEADS * SUBLANES // KV_HEADS
    seq3 = lambda b, j, pt: (b, 0, 0)
    seq4 = lambda b, j, pt: (b, 0, 0, 0)

    def page(i):
        return lambda b, j, pt: (pt[b, n_pages - 1 - (n * j + i)], 0, 0)

    in_specs = [pl.BlockSpec((None, KV_HEADS, rows, HEAD_DIM), seq4), pl.BlockSpec((None, KV_HEADS, PAGE, HEAD_DIM), seq4),
                pl.BlockSpec((None, KV_HEADS, PAGE, HEAD_DIM), seq4), pl.BlockSpec((None, N_HEADS, PAGE), seq3)]
    in_specs += [pl.BlockSpec((None, PAGE * KV_HEADS, HEAD_DIM), page(i)) for i in range(n)]
    in_specs += [pl.BlockSpec((None, PAGE * KV_HEADS, HEAD_DIM), page(i)) for i in range(n)]
    in_specs += [pl.BlockSpec((None, N_HEADS, PAGE), page(i)) for i in range(n)]
    grid_spec = pltpu.PrefetchScalarGridSpec(
        num_scalar_prefetch=1, grid=(db, n_pages // n), in_specs=in_specs,
        out_specs=pl.BlockSpec((None, LANES, HEAD_DIM), seq3),
        scratch_shapes=[pltpu.VMEM((LANES, 1), F32), pltpu.VMEM((LANES, 1), F32), pltpu.VMEM((LANES, HEAD_DIM), F32),
                        pltpu.VMEM((N_HEADS, 1), F32), pltpu.VMEM((LANES, 1), F32)])
    return pl.pallas_call(
        functools.partial(_fox_sample_kernel, pages_per_step=n), grid_spec=grid_spec,
        out_shape=jax.ShapeDtypeStruct((db, LANES, HEAD_DIM), BF16),
        compiler_params=_params("parallel", "arbitrary"), name="fox_sample_attn")(
            page_table, q4, knew4, vnew4, lfnew, *([cache_k] * n), *([cache_v] * n), *([cache_lf_t] * n))


def _swa_sample_kernel(qbd_ref, kbuf_ref, vbuf_ref, knew_ref, vnew_ref, sink_ref, o_ref, m_s, l_s, acc_s):
    qbd = qbd_ref[...]
    row_q = lax.broadcasted_iota(jnp.int32, (LANES, LANES), 0) % SUBLANES
    lane = lax.broadcasted_iota(jnp.int32, (LANES, LANES), 1)
    m_s[...] = jnp.full_like(m_s, NEG)
    l_s[...] = jnp.zeros_like(l_s)
    acc_s[...] = jnp.zeros_like(acc_s)
    _decode_update(qbd, kbuf_ref[...], vbuf_ref[...], None, lane > row_q, m_s, l_s, acc_s)
    _decode_update(qbd, knew_ref[...], vnew_ref[...], None, lane <= row_q, m_s, l_s, acc_s)
    l = l_s[...] + jnp.exp(sink_ref[...] - m_s[...])
    o_ref[...] = (acc_s[...] * (1.0 / l)).astype(BF16)


def _swa_sample(qbd, kbuf, vbuf, knew, vnew, sink_rows):
    db = qbd.shape[0]
    seq = lambda b: (b, 0, 0)
    blk = pl.BlockSpec((None, LANES, D_KV), seq)
    return pl.pallas_call(
        _swa_sample_kernel, grid=(db,),
        in_specs=[blk, blk, blk, blk, blk, pl.BlockSpec((LANES, 1), lambda b: (0, 0))],
        out_specs=blk, out_shape=jax.ShapeDtypeStruct((db, LANES, D_KV), BF16),
        scratch_shapes=[pltpu.VMEM((LANES, 1), F32), pltpu.VMEM((LANES, 1), F32), pltpu.VMEM((LANES, D_KV), F32)],
        compiler_params=_params("parallel"), name="swa_sample_attn")(qbd, kbuf, vbuf, knew, vnew, sink_rows)


def _norm_kernel(x_ref, g_ref, h_ref):
    h_ref[...] = _rms(x_ref[...], g_ref[...])


def _norm(x, g, tm=512):
    t = x.shape[0]
    row = lambda i: (i, 0)
    return pl.pallas_call(
        _norm_kernel, grid=(t // tm,),
        in_specs=[pl.BlockSpec((tm, D_MODEL), row), pl.BlockSpec((1, D_MODEL), lambda i: (0, 0))],
        out_specs=pl.BlockSpec((tm, D_MODEL), row), out_shape=jax.ShapeDtypeStruct((t, D_MODEL), F32),
        compiler_params=_params("parallel"), name="pre_norm")(x, g)


def _rwkv_proj_kernel(h_ref, hp_ref, mu_ref, wr_ref, wk_ref, wv_ref, w1_ref, w2_ref, a1_ref, a2_ref, g1_ref, g2_ref,
                      vec_ref, gmat_ref, r_ref, w_ref, k_ref, v_ref, kk_ref, kka_ref, g_ref):
    h = h_ref[...]
    xx = hp_ref[...] - h

    def mix(i):
        return (h + xx * mu_ref[i:i + 1, :]).astype(BF16)

    w0, a0, k_k, k_a = (vec_ref[i:i + 1, :] for i in range(4))
    r = _dot(mix(0), wr_ref[...])
    k = _dot(mix(2), wk_ref[...])
    v = _dot(mix(3), wv_ref[...])
    lw = _dot(jnp.tanh(_dot(mix(1), w1_ref[...])).astype(BF16), w2_ref[...])
    w_log = -_softplus(-(w0 + lw)) - 0.5
    decay = jnp.exp(-jnp.exp(w_log))
    a = jax.nn.sigmoid(a0 + _dot(_dot(mix(4), a1_ref[...]).astype(BF16), a2_ref[...]))
    g = _dot(jax.nn.sigmoid(_dot(mix(5), g1_ref[...])).astype(BF16), g2_ref[...])
    kk = k * k_k
    ss = _split_dot(_tile_sum(kk * kk), gmat_ref[...], 3)
    kk = kk * _tile8(lax.rsqrt(jnp.maximum(ss, 1e-24)))
    r_ref[...] = r
    w_ref[...] = decay
    k_ref[...] = k * (1.0 + (a - 1.0) * k_a)
    v_ref[...] = v
    kk_ref[...] = kk
    kka_ref[...] = kk * a
    g_ref[...] = g


def _rwkv_proj(h, hp, mu, wr, wk, wv, w1, w2, a1, a2, g1, g2, vecs, gmat, tm=256):
    t = h.shape[0]
    row = lambda i: (i, 0)
    fixed = lambda i: (0, 0)
    full = lambda a: pl.BlockSpec(a.shape, fixed)
    tok = pl.BlockSpec((tm, D_MODEL), row)
    consts = [mu, wr, wk, wv, w1, w2, a1, a2, g1, g2, vecs, gmat]
    return pl.pallas_call(
        _rwkv_proj_kernel, grid=(t // tm,), in_specs=[tok, tok] + [full(a) for a in consts],
        out_specs=[tok] * 7, out_shape=[jax.ShapeDtypeStruct((t, D_MODEL), F32)] * 7,
        compiler_params=_params("parallel"), name="rwkv_proj")(h, hp, *consts)


def _rwkv_scan_kernel(*refs, nb, tc):
    ins = refs[:6 * nb]
    r_refs, w_refs, k_refs, v_refs, kk_refs, kka_refs = (ins[i * nb:(i + 1) * nb] for i in range(6))
    s0_ref, gmat_ref, y_ref, sout_ref, s_s = refs[6 * nb:]
    tt = pl.program_id(1)
    groups = HEAD_DIM // SUBLANES
    eye = (lax.broadcasted_iota(jnp.int32, (SUBLANES, LANES), 1) // N_HEADS
           == lax.broadcasted_iota(jnp.int32, (SUBLANES, LANES), 0))
    gmat = gmat_ref[...]

    @pl.when(tt == 0)
    def _():
        s_s[...] = s0_ref[...]

    def bcast(ref, t):
        return jnp.broadcast_to(ref[pl.ds(t, 1), :], (SUBLANES, D_MODEL))

    def step(t, _):
        kk = [bcast(kk_refs[b], t) for b in range(nb)]
        sa_parts = []
        v_parts = []
        for b in range(nb):
            vrow = bcast(v_refs[b], t)
            for g in range(groups):
                sa_parts.append(_tile_sum(s_s[b, g * SUBLANES:(g + 1) * SUBLANES, :] * kk[b]))
                v_parts.append(jnp.where(eye, vrow[:, g * LANES:(g + 1) * LANES], 0.0))
        sa_all = _split_dot(jnp.concatenate(sa_parts, axis=0), gmat, 2)
        v_all = _split_dot(jnp.concatenate(v_parts, axis=0), gmat, 2)
        y_parts = []
        for b in range(nb):
            w = bcast(w_refs[b], t)
            kka = bcast(kka_refs[b], t)
            k = bcast(k_refs[b], t)
            r = bcast(r_refs[b], t)
            for g in range(groups):
                lo = (b * groups + g) * SUBLANES
                sa = _tile8(-sa_all[lo:lo + SUBLANES, :])
                vv = _tile8(v_all[lo:lo + SUBLANES, :])
                s_new = s_s[b, g * SUBLANES:(g + 1) * SUBLANES, :] * w + sa * kka + vv * k
                s_s[b, g * SUBLANES:(g + 1) * SUBLANES, :] = s_new
                y_parts.append(_tile_sum(s_new * r))
        y_all = _split_dot(jnp.concatenate(y_parts, axis=0), gmat, 2)
        t8 = pl.multiple_of((t // SUBLANES) * SUBLANES, SUBLANES)
        mine = lax.broadcasted_iota(jnp.int32, (SUBLANES, LANES), 0) == t % SUBLANES
        for b in range(nb):
            for g in range(groups):
                lo = (b * groups + g) * SUBLANES
                yrow = jnp.sum(jnp.where(eye, y_all[lo:lo + SUBLANES, :], 0.0), axis=0, keepdims=True)
                cur = y_ref[b, pl.ds(t8, SUBLANES), g * LANES:(g + 1) * LANES]
                y_ref[b, pl.ds(t8, SUBLANES), g * LANES:(g + 1) * LANES] = jnp.where(
                    mine, jnp.broadcast_to(yrow, (SUBLANES, LANES)), cur)
        return 0

    y_ref[...] = jnp.zeros_like(y_ref)
    lax.fori_loop(0, tc, step, 0)

    @pl.when(tt == pl.num_programs(1) - 1)
    def _():
        sout_ref[...] = s_s[...]


def _rwkv_scan(vecs, s0, gmat, batch, seq, row0, nb, tc):
    nt = seq // tc
    blk0 = row0 // tc

    def tok(n):
        return pl.BlockSpec((tc, D_MODEL), lambda bb, tt: (blk0 + (bb * nb + n) * nt + tt, 0))

    in_specs = [tok(n) for _ in range(6) for n in range(nb)]
    args = [a for a in vecs for _ in range(nb)]
    state = pl.BlockSpec((nb, HEAD_DIM, D_MODEL), lambda bb, tt: (bb, 0, 0))
    in_specs += [state, pl.BlockSpec((LANES, LANES), lambda bb, tt: (0, 0))]
    return pl.pallas_call(
        functools.partial(_rwkv_scan_kernel, nb=nb, tc=tc), grid=(batch // nb, nt), in_specs=in_specs,
        out_specs=[pl.BlockSpec((nb, tc, D_MODEL), lambda bb, tt: (bb, tt, 0)), state],
        out_shape=[jax.ShapeDtypeStruct((batch, seq, D_MODEL), F32),
                   jax.ShapeDtypeStruct((batch, HEAD_DIM, D_MODEL), F32)],
        scratch_shapes=[pltpu.VMEM((nb, HEAD_DIM, D_MODEL), F32)],
        compiler_params=_params("parallel", "arbitrary"), name="rwkv_scan")(*args, s0, gmat)


def _rwkv_out_kernel(y_ref, r_ref, k_ref, v_ref, g_ref, x_ref, vec_ref, gmat_ref, wo_ref, gpost_ref, o_ref):
    r_k, ln_w, ln_b = (vec_ref[i:i + 1, :] for i in range(3))
    gmat = gmat_ref[...]
    y = y_ref[...]
    mean = _split_dot(_tile_sum(y), gmat, 3) * (1.0 / HEAD_DIM)
    yc = y - _tile8(mean)
    var = _split_dot(_tile_sum(yc * yc), gmat, 3) * (1.0 / HEAD_DIM)
    yn = yc * _tile8(lax.rsqrt(var + GN_EPS)) * ln_w + ln_b
    bonus = _split_dot(_tile_sum(r_ref[...] * k_ref[...] * r_k), gmat, 3)
    out = (yn + _tile8(bonus) * v_ref[...]) * g_ref[...]
    m = _dot(out.astype(BF16), wo_ref[...])
    o_ref[...] = x_ref[...] + _rms(m, gpost_ref[...])


def _rwkv_out(y, r, k, v, g, x, vecs, gmat, wo, gpost, tm=256):
    t = x.shape[0]
    row = lambda i: (i, 0)
    fixed = lambda i: (0, 0)
    tok = pl.BlockSpec((tm, D_MODEL), row)
    consts = [vecs, gmat, wo, gpost]
    return pl.pallas_call(
        _rwkv_out_kernel, grid=(t // tm,), in_specs=[tok] * 6 + [pl.BlockSpec(a.shape, fixed) for a in consts],
        out_specs=tok, out_shape=jax.ShapeDtypeStruct((t, D_MODEL), F32),
        compiler_params=_params("parallel"), name="rwkv_out")(y, r, k, v, g, x, *consts)


def _block_diag_q(q_perm, db, ds):
    inv = np.argsort(np.asarray(HEAD_PERM))
    q = q_perm.reshape(db, ds, N_HEADS, HEAD_DIM)[:, :, inv]
    q = q.reshape(db, ds, KV_HEADS, N_HEADS // KV_HEADS, HEAD_DIM).transpose(0, 2, 3, 1, 4)
    eye = jnp.eye(KV_HEADS, dtype=q.dtype)
    qbd = q[:, :, :, :, None, :] * eye[None, :, None, None, :, None]
    return qbd.reshape(db, N_HEADS * ds, D_KV)


def _undo_block_diag(o, db, ds):
    o = o.reshape(db, KV_HEADS, N_HEADS // KV_HEADS, ds, KV_HEADS, HEAD_DIM)
    o = jnp.stack([o[:, kv, :, :, kv, :] for kv in range(KV_HEADS)], axis=1)
    o = o.transpose(0, 3, 1, 2, 4).reshape(db, ds, N_HEADS, HEAD_DIM)
    return o[:, :, np.asarray(HEAD_PERM)].reshape(db * ds, D_MODEL)


def _group_q(q_perm, db, ds):
    inv = np.argsort(np.asarray(HEAD_PERM))
    q = q_perm.reshape(db, ds, N_HEADS, HEAD_DIM)[:, :, inv]
    q = q.reshape(db, ds, KV_HEADS, N_HEADS // KV_HEADS, HEAD_DIM).transpose(0, 2, 3, 1, 4)
    return q.reshape(db, KV_HEADS, N_HEADS // KV_HEADS * ds, HEAD_DIM)


def _ungroup_o(o, db, ds):
    o = o.reshape(db, N_HEADS, ds, HEAD_DIM).transpose(0, 2, 1, 3)
    return o[:, :, np.asarray(HEAD_PERM)].reshape(db * ds, D_MODEL)


def _pad_page(x, db, ds):
    return jnp.pad(x.reshape(db, ds, -1), ((0, 0), (0, PAGE - ds), (0, 0)))


def _rope_tables(pos):
    inv_freq = ROPE_THETA ** (-jnp.arange(0, ROT_DIM, 2, dtype=F32) / ROT_DIM)
    ang = pos.astype(F32)[:, None] * inv_freq[None, :]
    cos, sin = jnp.cos(ang), jnp.sin(ang)
    half = ROT_DIM // 2
    ones = jnp.ones((pos.shape[0], HEAD_DIM - ROT_DIM), F32)
    zeros = jnp.zeros((pos.shape[0], HEAD_DIM - ROT_DIM), F32)
    zh = jnp.zeros_like(sin)
    a = jnp.concatenate([cos, cos, ones], axis=1)
    m = jnp.concatenate([-sin, zh, zeros], axis=1)
    p = jnp.concatenate([zh, sin, zeros], axis=1)
    rep = LANES // HEAD_DIM
    return tuple(jnp.tile(z, (1, rep)) for z in (a, m, p))


def kernel(x_prompt, x_sample, cache_fox_k_l0, cache_fox_v_l0, cache_fox_logf_l0, state_rwkv_wkv_l1, state_rwkv_shift_l1, cache_swa_k_l2, cache_swa_v_l2, cache_fox_k_l3, cache_fox_v_l3, cache_fox_logf_l3, page_table, p_prompt, p_sample, norm_mix_pre, norm_mix_post, norm_ffn_pre, norm_ffn_post, ffn_w_up, ffn_w_down, ple_w_proj, ple_w_gate, fox_w_q, fox_w_k, fox_w_v, fox_w_f, fox_b_f, fox_w_o, rwkv_mu, rwkv_w_r, rwkv_w_k, rwkv_w_v, rwkv_w_o, rwkv_w0, rwkv_w1, rwkv_w2, rwkv_a0, rwkv_a1, rwkv_a2, rwkv_g1, rwkv_g2, rwkv_k_k, rwkv_k_a, rwkv_r_k, rwkv_ln_w, rwkv_ln_b, swa_w_q, swa_b_q, swa_w_k, swa_b_k, swa_w_v, swa_b_v, swa_sinks, swa_w_o):
    bp, sp, _ = x_prompt.shape
    db, ds, _ = x_sample.shape
    depth = norm_mix_pre.shape[0]
    tp, ts = bp * sp, db * ds
    n_pages = page_table.shape[1]
    past = n_pages * PAGE
    bf = lambda a: a.astype(BF16)
    vec = lambda a: a.reshape(1, -1).astype(F32)

    x = jnp.concatenate([x_prompt.reshape(tp, D_MODEL), x_sample.reshape(ts, D_MODEL)], axis=0)
    p_all = jnp.concatenate([p_prompt.reshape(depth, tp, D_PLE), p_sample.reshape(depth, ts, D_PLE)], axis=1)
    pos = jnp.concatenate([jnp.tile(jnp.arange(sp), bp), jnp.tile(past + jnp.arange(ds), db)])
    gmat = (np.arange(LANES)[:, None] % N_HEADS == np.arange(LANES)[None, :] % N_HEADS)
    gmat = jnp.asarray(gmat, BF16)
    fox_caches = ((cache_fox_k_l0, cache_fox_v_l0, cache_fox_logf_l0), (cache_fox_k_l3, cache_fox_v_l3, cache_fox_logf_l3))
    new_state = []

    for i in range(depth):
        kind, j = i % N_MIXERS, i // N_MIXERS
        g_pre = vec(norm_mix_pre[i])
        g_post = vec(norm_mix_post[i])
        if kind == 0:
            wf = jnp.pad(fox_w_f[j], ((0, 0), (0, LANES - N_HEADS)))
            w = bf(jnp.concatenate([fox_w_q[j][:, Q_COLS], fox_w_k[j], fox_w_v[j], wf], axis=1))
            b = jnp.concatenate([jnp.zeros((D_MODEL + 2 * D_KV,), F32), fox_b_f[j], jnp.zeros((LANES - N_HEADS,), F32)])
            q, k, v, lf = _proj(x, g_pre, w, vec(b), None)
            lf = lf[:, :N_HEADS]
            lf_p = lf[:tp].reshape(bp, sp, N_HEADS)
            ck = _cumsum(lf_p.transpose(0, 2, 1))
            cq = ck.transpose(0, 2, 1).reshape(tp, N_HEADS)
            o_p = _flash(q, k, v, cq, ck, bp, sp)
            ck_cache, cv_cache, clf_cache = fox_caches[j]
            n_pool = ck_cache.shape[0]
            kv_page = lambda z: _pad_page(z, db, ds).reshape(db, PAGE, KV_HEADS, HEAD_DIM).transpose(0, 2, 1, 3)
            o_s = _fox_sample(
                page_table, _group_q(q[tp:], db, ds), kv_page(k[tp:]), kv_page(v[tp:]),
                _pad_page(lf[tp:], db, ds).transpose(0, 2, 1), ck_cache.reshape(n_pool, PAGE * KV_HEADS, HEAD_DIM),
                cv_cache.reshape(n_pool, PAGE * KV_HEADS, HEAD_DIM), clf_cache.transpose(0, 2, 1))
            o = jnp.concatenate([o_p, _ungroup_o(o_s, db, ds)], axis=0)
            x = _oproj(o, bf(fox_w_o[j][Q_COLS, :]), g_post, x)
            new_state.append((k[:tp].reshape(bp, sp, KV_HEADS, HEAD_DIM), v[:tp].reshape(bp, sp, KV_HEADS, HEAD_DIM),
                              lf_p, k[tp:].reshape(db, ds, KV_HEADS, HEAD_DIM),
                              v[tp:].reshape(db, ds, KV_HEADS, HEAD_DIM), lf[tp:].reshape(db, ds, N_HEADS)))
        elif kind == 1:
            h = _norm(x, g_pre)
            h_p = h[:tp].reshape(bp, sp, D_MODEL)
            h_s = h[tp:].reshape(db, ds, D_MODEL)
            hp = jnp.concatenate([
                jnp.concatenate([jnp.zeros((bp, 1, D_MODEL), F32), h_p[:, :-1]], axis=1).reshape(tp, D_MODEL),
                jnp.concatenate([state_rwkv_shift_l1[:, None, :], h_s[:, :-1]], axis=1).reshape(ts, D_MODEL)], axis=0)
            pc = lambda a: a[:, R_COLS]
            vecs = jnp.stack([rwkv_w0[j][R_COLS], rwkv_a0[j][R_COLS], rwkv_k_k[j][R_COLS], rwkv_k_a[j][R_COLS]])
            r, w, k, v, kk, kka, g = _rwkv_proj(
                h, hp, rwkv_mu[j], bf(pc(rwkv_w_r[j])), bf(pc(rwkv_w_k[j])), bf(pc(rwkv_w_v[j])), bf(rwkv_w1[j]),
                bf(pc(rwkv_w2[j])), bf(rwkv_a1[j]), bf(pc(rwkv_a2[j])), bf(rwkv_g1[j]), bf(pc(rwkv_g2[j])), vecs, gmat)
            scan_in = (r, w, k, v, kk, kka)
            to_lanes = lambda s: s.transpose(0, 2, 3, 1).reshape(s.shape[0], HEAD_DIM, D_MODEL)
            from_lanes = lambda s: s.reshape(s.shape[0], HEAD_DIM, HEAD_DIM, N_HEADS).transpose(0, 3, 1, 2)
            y_p, s_p = _rwkv_scan(scan_in, jnp.zeros((bp, HEAD_DIM, D_MODEL), F32), gmat, bp, sp, 0, bp, 64)
            y_s, s_s = _rwkv_scan(scan_in, to_lanes(state_rwkv_wkv_l1.astype(F32)), gmat, db, ds, tp, 4, ds)
            y = jnp.concatenate([y_p.reshape(tp, D_MODEL), y_s.reshape(ts, D_MODEL)], axis=0)
            ovecs = jnp.stack([rwkv_r_k[j].reshape(-1)[R_COLS], rwkv_ln_w[j][R_COLS], rwkv_ln_b[j][R_COLS]])
            x = _rwkv_out(y, r, k, v, g, x, ovecs, gmat, bf(rwkv_w_o[j][R_COLS, :]), g_post)
            new_state.append((from_lanes(s_p), h_p[:, -1], from_lanes(s_s), h_s[:, -1]))
        else:
            w = bf(jnp.concatenate([swa_w_q[j][:, Q_COLS], swa_w_k[j], swa_w_v[j]], axis=1))
            b = jnp.concatenate([swa_b_q[j][Q_COLS], swa_b_k[j], swa_b_v[j]])
            q, k, v = _proj(x, g_pre, w, vec(b), _rope_tables(pos))
            o_p = _swa_band(q, k, v, vec(swa_sinks[j]), bp, sp)
            keep = cache_swa_k_l2.shape[1]
            k_s = k[tp:].reshape(db, ds, D_KV)
            v_s = v[tp:].reshape(db, ds, D_KV)
            kbuf = cache_swa_k_l2.reshape(db, keep, D_KV)
            vbuf = cache_swa_v_l2.reshape(db, keep, D_KV)
            sink_rows = jnp.repeat(swa_sinks[j].astype(F32), ds).reshape(N_HEADS * ds, 1)
            o_s = _swa_sample(_block_diag_q(q[tp:], db, ds), kbuf, vbuf, _pad_page(k[tp:], db, ds),
                              _pad_page(v[tp:], db, ds), sink_rows)
            o = jnp.concatenate([o_p, _undo_block_diag(o_s, db, ds)], axis=0)
            x = _oproj(o, bf(swa_w_o[j][Q_COLS, :]), g_post, x)
            k_p = k[:tp].reshape(bp, sp, KV_HEADS, HEAD_DIM)
            v_p = v[:tp].reshape(bp, sp, KV_HEADS, HEAD_DIM)
            wk = min(WINDOW, sp)
            new_state.append((k_p[:, sp - wk:], v_p[:, sp - wk:],
                              jnp.concatenate([kbuf, k_s], axis=1)[:, ds:].reshape(db, keep, KV_HEADS, HEAD_DIM),
                              jnp.concatenate([vbuf, v_s], axis=1)[:, ds:].reshape(db, keep, KV_HEADS, HEAD_DIM)))
        x = _ffn(x, vec(norm_ffn_pre[i]), bf(ffn_w_up[i]), bf(ffn_w_down[i]), vec(norm_ffn_post[i]), p_all[i],
                 bf(ple_w_proj[i]), bf(ple_w_gate[i]))

    outs = [x[:tp].reshape(bp, sp, D_MODEL), x[tp:].reshape(db, ds, D_MODEL)]
    for st in new_state:
        outs.extend(st)
    return tuple(outs)
```

```python
import functools

import numpy as np
import jax
import jax.numpy as jnp
from jax import lax
from jax.experimental import pallas as pl
from jax.experimental.pallas import tpu as pltpu

F32 = jnp.float32
BF16 = jnp.bfloat16

D_MODEL = 1024
HEAD_DIM = 64
N_HEADS = 16
KV_HEADS = 4
D_KV = KV_HEADS * HEAD_DIM
D_FF = 4096
D_PLE = 256
PAGE = 128
WINDOW = 128
ROT_DIM = 16
ROPE_THETA = 500000.0
NORM_EPS = 1e-6
GN_EPS = 64e-5
N_MIXERS = 3
LANES = 128
SUBLANES = 8
NEG = -1e30
VMEM_LIMIT = 48 * 1024 * 1024
SCALE = HEAD_DIM ** -0.5

HEAD_PERM = (0, 4, 1, 5, 2, 6, 3, 7, 8, 12, 9, 13, 10, 14, 11, 15)
Q_COLS = np.concatenate([np.arange(HEAD_DIM) + HEAD_DIM * h for h in HEAD_PERM])
R_COLS = (np.arange(D_MODEL) % N_HEADS) * HEAD_DIM + np.arange(D_MODEL) // N_HEADS


def _params(*sem):
    return pltpu.CompilerParams(dimension_semantics=sem, vmem_limit_bytes=VMEM_LIMIT)


def _dot(a, b):
    return jnp.dot(a, b, preferred_element_type=F32)


def _dot_nt(a, b):
    return lax.dot_general(a, b, (((1,), (1,)), ((), ())), preferred_element_type=F32)


def _rms(x, g):
    return x * lax.rsqrt(jnp.mean(x * x, axis=-1, keepdims=True) + NORM_EPS) * g


def _log_sigmoid(x):
    return jnp.minimum(x, 0.0) - jnp.log1p(jnp.exp(-jnp.abs(x)))


def _softplus(x):
    return jnp.maximum(x, 0.0) + jnp.log1p(jnp.exp(-jnp.abs(x)))


def _split_dot(x, g, parts):
    out = None
    for _ in range(parts):
        piece = x.astype(BF16)
        term = _dot(piece, g)
        out = term if out is None else out + term
        x = x - piece.astype(F32)
    return out


def _tile_sum(x):
    out = x[:, 0:LANES]
    for t in range(1, D_MODEL // LANES):
        out = out + x[:, t * LANES:(t + 1) * LANES]
    return out


def _tile8(x):
    return jnp.concatenate([x] * (D_MODEL // LANES), axis=1)


def _rope(x, a, m, p):
    chunks = []
    for c in range(x.shape[1] // LANES):
        xc = x[:, c * LANES:(c + 1) * LANES]
        chunks.append(xc * a + pltpu.roll(xc, LANES - ROT_DIM // 2, 1) * m + pltpu.roll(xc, ROT_DIM // 2, 1) * p)
    return jnp.concatenate(chunks, axis=1)


def _proj_kernel(x_ref, g_ref, w_ref, b_ref, *rest, rope):
    if rope:
        ra_ref, rm_ref, rp_ref, q_ref, k_ref, v_ref = rest
    else:
        q_ref, k_ref, v_ref, lf_ref = rest
    h = _rms(x_ref[...], g_ref[...]).astype(BF16)
    res = _dot(h, w_ref[...]) + b_ref[...]
    q = res[:, :D_MODEL]
    k = res[:, D_MODEL:D_MODEL + D_KV]
    v = res[:, D_MODEL + D_KV:D_MODEL + 2 * D_KV]
    if rope:
        a, m, p = ra_ref[...], rm_ref[...], rp_ref[...]
        q = _rope(q, a, m, p)
        k = _rope(k, a, m, p)
    else:
        lf_ref[...] = _log_sigmoid(res[:, D_MODEL + 2 * D_KV:])
    q_ref[...] = (q * SCALE).astype(BF16)
    k_ref[...] = k
    v_ref[...] = v


def _proj(x, g, w, b, rope_tabs, tm=512):
    t = x.shape[0]
    n = w.shape[1]
    rope = rope_tabs is not None
    row = lambda i: (i, 0)
    fixed = lambda i: (0, 0)
    in_specs = [pl.BlockSpec((tm, D_MODEL), row), pl.BlockSpec((1, D_MODEL), fixed),
                pl.BlockSpec((D_MODEL, n), fixed), pl.BlockSpec((1, n), fixed)]
    args = [x, g, w, b]
    out_shape = [jax.ShapeDtypeStruct((t, D_MODEL), BF16), jax.ShapeDtypeStruct((t, D_KV), F32),
                 jax.ShapeDtypeStruct((t, D_KV), F32)]
    out_specs = [pl.BlockSpec((tm, D_MODEL), row), pl.BlockSpec((tm, D_KV), row), pl.BlockSpec((tm, D_KV), row)]
    if rope:
        in_specs += [pl.BlockSpec((tm, LANES), row)] * 3
        args += list(rope_tabs)
    else:
        out_shape.append(jax.ShapeDtypeStruct((t, LANES), F32))
        out_specs.append(pl.BlockSpec((tm, LANES), row))
    return pl.pallas_call(
        functools.partial(_proj_kernel, rope=rope), grid=(t // tm,), in_specs=in_specs, out_specs=out_specs,
        out_shape=out_shape, compiler_params=_params("parallel"),
        name="swa_proj" if rope else "fox_proj")(*args)


def _oproj_kernel(o_ref, w_ref, g_ref, x_ref, y_ref):
    m = _dot(o_ref[...], w_ref[...])
    y_ref[...] = x_ref[...] + _rms(m, g_ref[...])


def _oproj(o, w, g, x, tm=512):
    t = x.shape[0]
    row = lambda i: (i, 0)
    fixed = lambda i: (0, 0)
    return pl.pallas_call(
        _oproj_kernel, grid=(t // tm,),
        in_specs=[pl.BlockSpec((tm, D_MODEL), row), pl.BlockSpec((D_MODEL, D_MODEL), fixed),
                  pl.BlockSpec((1, D_MODEL), fixed), pl.BlockSpec((tm, D_MODEL), row)],
        out_specs=pl.BlockSpec((tm, D_MODEL), row), out_shape=jax.ShapeDtypeStruct((t, D_MODEL), F32),
        compiler_params=_params("parallel"), name="out_proj")(o, w, g, x)


def _ffn_kernel(x_ref, gpre_ref, wup_ref, wdn_ref, gpost_ref, p_ref, wp_ref, wg_ref, y_ref, h_s, acc_s):
    kf = pl.program_id(1)

    @pl.when(kf == 0)
    def _():
        h_s[...] = _rms(x_ref[...], gpre_ref[...]).astype(BF16)
        acc_s[...] = jnp.zeros_like(acc_s)

    u = jnp.maximum(_dot(h_s[...], wup_ref[...]), 0.0)
    acc_s[...] += _dot((u * u).astype(BF16), wdn_ref[...])

    @pl.when(kf == pl.num_programs(1) - 1)
    def _():
        x1 = x_ref[...] + _rms(acc_s[...], gpost_ref[...])
        gate = jax.nn.sigmoid(_dot(x1.astype(BF16), wg_ref[...]))
        y_ref[...] = x1 + _dot(p_ref[...].astype(BF16), wp_ref[...]) * gate


def _ffn(x, gpre, wup, wdn, gpost, p, wp, wg, tm=512, tf=1024):
    t = x.shape[0]
    row = lambda i, k: (i, 0)
    fixed = lambda i, k: (0, 0)
    return pl.pallas_call(
        _ffn_kernel, grid=(t // tm, D_FF // tf),
        in_specs=[pl.BlockSpec((tm, D_MODEL), row), pl.BlockSpec((1, D_MODEL), fixed),
                  pl.BlockSpec((D_MODEL, tf), lambda i, k: (0, k)), pl.BlockSpec((tf, D_MODEL), lambda i, k: (k, 0)),
                  pl.BlockSpec((1, D_MODEL), fixed), pl.BlockSpec((tm, D_PLE), row),
                  pl.BlockSpec((D_PLE, D_MODEL), fixed), pl.BlockSpec((D_MODEL, D_MODEL), fixed)],
        out_specs=pl.BlockSpec((tm, D_MODEL), row), out_shape=jax.ShapeDtypeStruct((t, D_MODEL), F32),
        scratch_shapes=[pltpu.VMEM((tm, D_MODEL), BF16), pltpu.VMEM((tm, D_MODEL), F32)],
        compiler_params=_params("parallel", "arbitrary"), name="ffn_ple")(x, gpre, wup, wdn, gpost, p, wp, wg)


def _cumsum_kernel(x_ref, c_ref):
    n = x_ref.shape[1]
    lane = lax.broadcasted_iota(jnp.int32, (N_HEADS, LANES), 1)
    carry = jnp.zeros((N_HEADS, 1), F32)
    for c in range(n // LANES):
        blk = x_ref[:, c * LANES:(c + 1) * LANES]
        sh = 1
        while sh < LANES:
            blk = blk + jnp.where(lane >= sh, pltpu.roll(blk, sh, 1), 0.0)
            sh *= 2
        blk = blk + carry
        c_ref[:, c * LANES:(c + 1) * LANES] = blk
        carry = blk[:, LANES - 1:LANES]


def _cumsum(x):
    b, h, t = x.shape
    spec = pl.BlockSpec((None, h, t), lambda i: (i, 0, 0))
    return pl.pallas_call(_cumsum_kernel, grid=(b,), in_specs=[spec], out_specs=spec,
                          out_shape=jax.ShapeDtypeStruct(x.shape, F32), compiler_params=_params("parallel"),
                          name="logf_cumsum")(x)


def _flash_kernel(qi_tab, ki_tab, first_tab, last_tab, q_ref, k_ref, v_ref, cq_ref, ck_ref, o_ref, qm_s, m_s, l_s, acc_s,
                  *, tq, tk):
    step = pl.program_id(1)
    qi = qi_tab[step]
    ki = ki_tab[step]
    lane = lax.broadcasted_iota(jnp.int32, (tq, LANES), 1)
    rep = tk // LANES

    @pl.when(first_tab[step] == 1)
    def _():
        m_s[...] = jnp.full_like(m_s, NEG)
        l_s[...] = jnp.zeros_like(l_s)
        acc_s[...] = jnp.zeros_like(acc_s)
        for pp in range(N_HEADS):
            qp = q_ref[:, (pp // 2) * LANES:(pp // 2 + 1) * LANES]
            keep = (lane < HEAD_DIM) if pp % 2 == 0 else (lane >= HEAD_DIM)
            qm_s[pp * tq:(pp + 1) * tq, :] = jnp.where(keep, qp, jnp.zeros_like(qp))

    kb = k_ref[...].astype(BF16)
    vb = v_ref[...].astype(BF16)
    row = qi * tq + lax.broadcasted_iota(jnp.int32, (tq, tk), 0)
    col = ki * tk + lax.broadcasted_iota(jnp.int32, (tq, tk), 1)
    mask = col <= row
    group = N_HEADS // 2
    for g0 in range(0, N_HEADS, group):
        kv = g0 // group
        kp = kb[:, kv * LANES:(kv + 1) * LANES]
        vp = vb[:, kv * LANES:(kv + 1) * LANES]
        rows = slice(g0 * tq, (g0 + group) * tq)
        s = _dot_nt(qm_s[rows, :], kp)
        parts = []
        for i in range(group):
            h = HEAD_PERM[g0 + i]
            sh = s[i * tq:(i + 1) * tq, :] - ck_ref[h:h + 1, :]
            parts.append(jnp.where(mask, sh + cq_ref[:, h:h + 1], NEG))
        s = jnp.concatenate(parts, axis=0)
        m_old = m_s[rows, :]
        m_new = jnp.maximum(m_old, jnp.max(s, axis=1, keepdims=True))
        alpha = jnp.exp(m_old - m_new)
        p = jnp.exp(s - jnp.concatenate([m_new] * rep, axis=1))
        l_s[rows, :] = alpha * l_s[rows, :] + jnp.sum(p, axis=1, keepdims=True)
        m_s[rows, :] = m_new
        pv = _dot(p.astype(BF16), vp)
        for i in range(0, group, 2):
            pair = (g0 + i) // 2
            a = acc_s[:, pair * LANES:(pair + 1) * LANES]
            a0, a1 = alpha[i * tq:(i + 1) * tq, :], alpha[(i + 1) * tq:(i + 2) * tq, :]
            acc_s[:, pair * LANES:(pair + 1) * LANES] = jnp.where(
                lane < HEAD_DIM, a0 * a + pv[i * tq:(i + 1) * tq, :], a1 * a + pv[(i + 1) * tq:(i + 2) * tq, :])

    @pl.when(last_tab[step] == 1)
    def _():
        for pair in range(N_HEADS // 2):
            l0 = l_s[(2 * pair) * tq:(2 * pair + 1) * tq, :]
            l1 = l_s[(2 * pair + 1) * tq:(2 * pair + 2) * tq, :]
            inv = jnp.where(lane < HEAD_DIM, 1.0 / l0, 1.0 / l1)
            o_ref[:, pair * LANES:(pair + 1) * LANES] = (acc_s[:, pair * LANES:(pair + 1) * LANES] * inv).astype(BF16)


def _flash(q, k, v, cq, ck, batch, seq, tq=512):
    tk = tq
    nq = seq // tq
    pairs = [(a, b) for a in range(nq) for b in range(a + 1)]
    qi_tab = jnp.asarray([a for a, _ in pairs], jnp.int32)
    ki_tab = jnp.asarray([b for _, b in pairs], jnp.int32)
    first_tab = jnp.asarray([int(b == 0) for a, b in pairs], jnp.int32)
    last_tab = jnp.asarray([int(b == a) for a, b in pairs], jnp.int32)
    qmap = lambda b, s, qt, kt, ft, lt: (b * nq + qt[s], 0)
    kmap = lambda b, s, qt, kt, ft, lt: (b * nq + kt[s], 0)
    in_specs = [pl.BlockSpec((tq, D_MODEL), qmap), pl.BlockSpec((tk, D_KV), kmap), pl.BlockSpec((tk, D_KV), kmap),
                pl.BlockSpec((tq, N_HEADS), qmap),
                pl.BlockSpec((None, N_HEADS, tk), lambda b, s, qt, kt, ft, lt: (b, 0, kt[s]))]
    grid_spec = pltpu.PrefetchScalarGridSpec(
        num_scalar_prefetch=4, grid=(batch, len(pairs)), in_specs=in_specs,
        out_specs=pl.BlockSpec((tq, D_MODEL), qmap),
        scratch_shapes=[pltpu.VMEM((N_HEADS * tq, LANES), BF16), pltpu.VMEM((N_HEADS * tq, LANES), F32),
                        pltpu.VMEM((N_HEADS * tq, LANES), F32), pltpu.VMEM((tq, D_MODEL), F32)])
    return pl.pallas_call(
        functools.partial(_flash_kernel, tq=tq, tk=tk), grid_spec=grid_spec,
        out_shape=jax.ShapeDtypeStruct((batch * seq, D_MODEL), BF16),
        compiler_params=_params("parallel", "arbitrary"),
        name="fox_prompt_attn")(qi_tab, ki_tab, first_tab, last_tab, q, k, v, cq, ck)


def _swa_band_kernel(q_ref, kp_ref, ko_ref, vp_ref, vo_ref, sink_ref, o_ref, *, tq):
    qi = pl.program_id(1)
    nk = WINDOW + tq
    kb = jnp.concatenate([kp_ref[...], ko_ref[...]], axis=0).astype(BF16)
    vb = jnp.concatenate([vp_ref[...], vo_ref[...]], axis=0).astype(BF16)
    lane = lax.broadcasted_iota(jnp.int32, (tq, LANES), 1)
    a = lax.broadcasted_iota(jnp.int32, (tq, nk), 0)
    c = lax.broadcasted_iota(jnp.int32, (tq, nk), 1)
    mask = (c > a) & (c <= a + WINDOW) & (qi * tq + c >= WINDOW)
    for pair in range(N_HEADS // 2):
        kv = pair // 4
        kp = kb[:, kv * LANES:(kv + 1) * LANES]
        vp = vb[:, kv * LANES:(kv + 1) * LANES]
        qp = q_ref[:, pair * LANES:(pair + 1) * LANES]
        outs = []
        for half in range(2):
            h = HEAD_PERM[2 * pair + half]
            keep = (lane < HEAD_DIM) if half == 0 else (lane >= HEAD_DIM)
            s = _dot_nt(jnp.where(keep, qp, jnp.zeros_like(qp)), kp)
            s = jnp.where(mask, s, NEG)
            sink = sink_ref[:, h:h + 1]
            m = jnp.maximum(jnp.max(s, axis=1, keepdims=True), sink)
            p = jnp.exp(s - m)
            l = jnp.sum(p, axis=1, keepdims=True) + jnp.exp(sink - m)
            outs.append(_dot(p.astype(BF16), vp) * (1.0 / l))
        o_ref[:, pair * LANES:(pair + 1) * LANES] = jnp.where(lane < HEAD_DIM, outs[0], outs[1]).astype(BF16)


def _swa_band(q, k, v, sinks, batch, seq, tq=256):
    nq = seq // tq
    r = tq // WINDOW
    qmap = lambda b, i: (b * nq + i, 0)
    pmap = lambda b, i: (jnp.maximum((b * nq + i) * r - 1, 0), 0)
    return pl.pallas_call(
        functools.partial(_swa_band_kernel, tq=tq), grid=(batch, nq),
        in_specs=[pl.BlockSpec((tq, D_MODEL), qmap), pl.BlockSpec((WINDOW, D_KV), pmap), pl.BlockSpec((tq, D_KV), qmap),
                  pl.BlockSpec((WINDOW, D_KV), pmap), pl.BlockSpec((tq, D_KV), qmap),
                  pl.BlockSpec((1, N_HEADS), lambda b, i: (0, 0))],
        out_specs=pl.BlockSpec((tq, D_MODEL), qmap), out_shape=jax.ShapeDtypeStruct((batch * seq, D_MODEL), BF16),
        compiler_params=_params("parallel", "parallel"), name="swa_prompt_attn")(q, k, k, v, v, sinks)


def _expand_heads(x16):
    return jnp.concatenate([jnp.broadcast_to(x16[h:h + 1, :], (SUBLANES, LANES)) for h in range(N_HEADS)], axis=0)


def _decode_update(qbd, k, v, bias, mask, m_s, l_s, acc_s):
    s = _dot_nt(qbd, k.astype(BF16))
    if bias is not None:
        s = s + bias
    if mask is not None:
        s = jnp.where(mask, s, NEG)
    m_old = m_s[...]
    m_new = jnp.maximum(m_old, jnp.max(s, axis=1, keepdims=True))
    alpha = jnp.exp(m_old - m_new)
    p = jnp.exp(s - m_new)
    l_s[...] = alpha * l_s[...] + jnp.sum(p, axis=1, keepdims=True)
    m_s[...] = m_new
    acc_s[...] = alpha * acc_s[...] + _dot(p.astype(BF16), v.astype(BF16))


def _suffix_sums(lf):
    lane = lax.broadcasted_iota(jnp.int32, lf.shape, 1)
    x = lf
    sh = 1
    while sh < LANES:
        x = x + jnp.where(lane < LANES - sh, pltpu.roll(x, LANES - sh, 1), 0.0)
        sh *= 2
    return x


def _fox_sample_kernel(pt_ref, q_ref, knew_ref, vnew_ref, lfnew_ref, *rest, pages_per_step):
    n = pages_per_step
    k_refs, v_refs, lf_refs = rest[:n], rest[n:2 * n], rest[2 * n:3 * n]
    o_ref, m_s, l_s, acc_s, carry_s, rowc_s = rest[3 * n:]
    j = pl.program_id(1)
    row_q = lax.broadcasted_iota(jnp.int32, (LANES, LANES), 0) % SUBLANES
    lane = lax.broadcasted_iota(jnp.int32, (LANES, LANES), 1)

    def update(kt, vt, bias, mask):
        s = _dot(q_ref[...], kt) + bias
        if mask is not None:
            s = jnp.where(mask, s, NEG)
        m_old = m_s[...]
        m_new = jnp.maximum(m_old, jnp.max(s, axis=1, keepdims=True))
        alpha = jnp.exp(m_old - m_new)
        p = jnp.exp(s - m_new)
        l_s[...] = alpha * l_s[...] + jnp.sum(p, axis=1, keepdims=True)
        m_s[...] = m_new
        acc_s[...] = alpha * acc_s[...] + _dot_nt(p.astype(BF16), vt)

    @pl.when(j == 0)
    def _():
        m_s[...] = jnp.full_like(m_s, NEG)
        l_s[...] = jnp.zeros_like(l_s)
        acc_s[...] = jnp.zeros_like(acc_s)
        lf = lfnew_ref[...]
        inc = _suffix_sums(lf)
        bias = _expand_heads(inc - lf)
        rowc = -jnp.sum(jnp.where(lane == row_q, bias, 0.0), axis=1, keepdims=True)
        rowc_s[...] = rowc
        carry_s[...] = inc[:, 0:1]
        update(knew_ref[...].astype(BF16), vnew_ref[...].astype(BF16), bias + rowc, lane <= row_q)

    carry = carry_s[...]
    biases = []
    for i in range(n):
        lf = lf_refs[i][...]
        inc = _suffix_sums(lf)
        biases.append(_expand_heads(inc - lf + carry))
        carry = carry + inc[:, 0:1]
    carry_s[...] = carry
    kt = jnp.concatenate([r[...].astype(BF16) for r in k_refs], axis=1)
    vt = jnp.concatenate([r[...].astype(BF16) for r in v_refs], axis=1)
    update(kt, vt, jnp.concatenate(biases, axis=1) + rowc_s[...], None)

    @pl.when(j == pl.num_programs(1) - 1)
    def _():
        o_ref[...] = (acc_s[...] * (1.0 / l_s[...])).astype(BF16)


def _fox_sample(page_table, qbd, knew_t, vnew_t, lfnew, cache_kt, cache_vt, cache_lf_t, pages_per_step=16):
    db, n_pages = page_table.shape
    n = pages_per_step
    seq = lambda b, j, pt: (b, 0, 0)

    def page(i):
        return lambda b, j, pt: (pt[b, n_pages - 1 - (n * j + i)], 0, 0)

    in_specs = [pl.BlockSpec((None, LANES, D_KV), seq), pl.BlockSpec((None, D_KV, PAGE), seq),
                pl.BlockSpec((None, D_KV, PAGE), seq), pl.BlockSpec((None, N_HEADS, PAGE), seq)]
    in_specs += [pl.BlockSpec((None, D_KV, PAGE), page(i)) for i in range(n)]
    in_specs += [pl.BlockSpec((None, D_KV, PAGE), page(i)) for i in range(n)]
    in_specs += [pl.BlockSpec((None, N_HEADS, PAGE), page(i)) for i in range(n)]
    grid_spec = pltpu.PrefetchScalarGridSpec(
        num_scalar_prefetch=1, grid=(db, n_pages // n), in_specs=in_specs,
        out_specs=pl.BlockSpec((None, LANES, D_KV), seq),
        scratch_shapes=[pltpu.VMEM((LANES, 1), F32), pltpu.VMEM((LANES, 1), F32), pltpu.VMEM((LANES, D_KV), F32),
                        pltpu.VMEM((N_HEADS, 1), F32), pltpu.VMEM((LANES, 1), F32)])
    return pl.pallas_call(
        functools.partial(_fox_sample_kernel, pages_per_step=n), grid_spec=grid_spec,
        out_shape=jax.ShapeDtypeStruct((db, LANES, D_KV), BF16),
        compiler_params=_params("parallel", "arbitrary"), name="fox_sample_attn")(
            page_table, qbd, knew_t, vnew_t, lfnew, *([cache_kt] * n), *([cache_vt] * n), *([cache_lf_t] * n))


def _swa_sample_kernel(qbd_ref, kbuf_ref, vbuf_ref, knew_ref, vnew_ref, sink_ref, o_ref, m_s, l_s, acc_s):
    qbd = qbd_ref[...]
    row_q = lax.broadcasted_iota(jnp.int32, (LANES, LANES), 0) % SUBLANES
    lane = lax.broadcasted_iota(jnp.int32, (LANES, LANES), 1)
    m_s[...] = jnp.full_like(m_s, NEG)
    l_s[...] = jnp.zeros_like(l_s)
    acc_s[...] = jnp.zeros_like(acc_s)
    _decode_update(qbd, kbuf_ref[...], vbuf_ref[...], None, lane > row_q, m_s, l_s, acc_s)
    _decode_update(qbd, knew_ref[...], vnew_ref[...], None, lane <= row_q, m_s, l_s, acc_s)
    l = l_s[...] + jnp.exp(sink_ref[...] - m_s[...])
    o_ref[...] = (acc_s[...] * (1.0 / l)).astype(BF16)


def _swa_sample(qbd, kbuf, vbuf, knew, vnew, sink_rows):
    db = qbd.shape[0]
    seq = lambda b: (b, 0, 0)
    blk = pl.BlockSpec((None, LANES, D_KV), seq)
    return pl.pallas_call(
        _swa_sample_kernel, grid=(db,),
        in_specs=[blk, blk, blk, blk, blk, pl.BlockSpec((LANES, 1), lambda b: (0, 0))],
        out_specs=blk, out_shape=jax.ShapeDtypeStruct((db, LANES, D_KV), BF16),
        scratch_shapes=[pltpu.VMEM((LANES, 1), F32), pltpu.VMEM((LANES, 1), F32), pltpu.VMEM((LANES, D_KV), F32)],
        compiler_params=_params("parallel"), name="swa_sample_attn")(qbd, kbuf, vbuf, knew, vnew, sink_rows)


def _norm_kernel(x_ref, g_ref, h_ref):
    h_ref[...] = _rms(x_ref[...], g_ref[...])


def _norm(x, g, tm=512):
    t = x.shape[0]
    row = lambda i: (i, 0)
    return pl.pallas_call(
        _norm_kernel, grid=(t // tm,),
        in_specs=[pl.BlockSpec((tm, D_MODEL), row), pl.BlockSpec((1, D_MODEL), lambda i: (0, 0))],
        out_specs=pl.BlockSpec((tm, D_MODEL), row), out_shape=jax.ShapeDtypeStruct((t, D_MODEL), F32),
        compiler_params=_params("parallel"), name="pre_norm")(x, g)


def _rwkv_proj_kernel(h_ref, hp_ref, mu_ref, wr_ref, wk_ref, wv_ref, w1_ref, w2_ref, a1_ref, a2_ref, g1_ref, g2_ref,
                      vec_ref, gmat_ref, r_ref, w_ref, k_ref, v_ref, kk_ref, kka_ref, g_ref):
    h = h_ref[...]
    xx = hp_ref[...] - h

    def mix(i):
        return (h + xx * mu_ref[i:i + 1, :]).astype(BF16)

    w0, a0, k_k, k_a = (vec_ref[i:i + 1, :] for i in range(4))
    r = _dot(mix(0), wr_ref[...])
    k = _dot(mix(2), wk_ref[...])
    v = _dot(mix(3), wv_ref[...])
    lw = _dot(jnp.tanh(_dot(mix(1), w1_ref[...])).astype(BF16), w2_ref[...])
    w_log = -_softplus(-(w0 + lw)) - 0.5
    decay = jnp.exp(-jnp.exp(w_log))
    a = jax.nn.sigmoid(a0 + _dot(_dot(mix(4), a1_ref[...]).astype(BF16), a2_ref[...]))
    g = _dot(jax.nn.sigmoid(_dot(mix(5), g1_ref[...])).astype(BF16), g2_ref[...])
    kk = k * k_k
    ss = _split_dot(_tile_sum(kk * kk), gmat_ref[...], 3)
    kk = kk * _tile8(lax.rsqrt(jnp.maximum(ss, 1e-24)))
    r_ref[...] = r
    w_ref[...] = decay
    k_ref[...] = k * (1.0 + (a - 1.0) * k_a)
    v_ref[...] = v
    kk_ref[...] = kk
    kka_ref[...] = kk * a
    g_ref[...] = g


def _rwkv_proj(h, hp, mu, wr, wk, wv, w1, w2, a1, a2, g1, g2, vecs, gmat, tm=256):
    t = h.shape[0]
    row = lambda i: (i, 0)
    fixed = lambda i: (0, 0)
    full = lambda a: pl.BlockSpec(a.shape, fixed)
    tok = pl.BlockSpec((tm, D_MODEL), row)
    consts = [mu, wr, wk, wv, w1, w2, a1, a2, g1, g2, vecs, gmat]
    return pl.pallas_call(
        _rwkv_proj_kernel, grid=(t // tm,), in_specs=[tok, tok] + [full(a) for a in consts],
        out_specs=[tok] * 7, out_shape=[jax.ShapeDtypeStruct((t, D_MODEL), F32)] * 7,
        compiler_params=_params("parallel"), name="rwkv_proj")(h, hp, *consts)


def _rwkv_scan_kernel(*refs, nb, tc):
    ins = refs[:6 * nb]
    r_refs, w_refs, k_refs, v_refs, kk_refs, kka_refs = (ins[i * nb:(i + 1) * nb] for i in range(6))
    s0_ref, gmat_ref, y_ref, sout_ref, s_s = refs[6 * nb:]
    tt = pl.program_id(1)
    groups = HEAD_DIM // SUBLANES
    eye = (lax.broadcasted_iota(jnp.int32, (SUBLANES, LANES), 1) // N_HEADS
           == lax.broadcasted_iota(jnp.int32, (SUBLANES, LANES), 0))
    gmat = gmat_ref[...]

    @pl.when(tt == 0)
    def _():
        s_s[...] = s0_ref[...]

    def bcast(ref, t):
        return jnp.broadcast_to(ref[pl.ds(t, 1), :], (SUBLANES, D_MODEL))

    def step(t, _):
        kk = [bcast(kk_refs[b], t) for b in range(nb)]
        sa_parts = []
        v_parts = []
        for b in range(nb):
            vrow = bcast(v_refs[b], t)
            for g in range(groups):
                sa_parts.append(_tile_sum(s_s[b, g * SUBLANES:(g + 1) * SUBLANES, :] * kk[b]))
                v_parts.append(jnp.where(eye, vrow[:, g * LANES:(g + 1) * LANES], 0.0))
        sa_all = _split_dot(jnp.concatenate(sa_parts, axis=0), gmat, 2)
        v_all = _split_dot(jnp.concatenate(v_parts, axis=0), gmat, 2)
        y_parts = []
        for b in range(nb):
            w = bcast(w_refs[b], t)
            kka = bcast(kka_refs[b], t)
            k = bcast(k_refs[b], t)
            r = bcast(r_refs[b], t)
            for g in range(groups):
                lo = (b * groups + g) * SUBLANES
                sa = _tile8(-sa_all[lo:lo + SUBLANES, :])
                vv = _tile8(v_all[lo:lo + SUBLANES, :])
                s_new = s_s[b, g * SUBLANES:(g + 1) * SUBLANES, :] * w + sa * kka + vv * k
                s_s[b, g * SUBLANES:(g + 1) * SUBLANES, :] = s_new
                y_parts.append(_tile_sum(s_new * r))
        y_all = _split_dot(jnp.concatenate(y_parts, axis=0), gmat, 2)
        t8 = pl.multiple_of((t // SUBLANES) * SUBLANES, SUBLANES)
        mine = lax.broadcasted_iota(jnp.int32, (SUBLANES, LANES), 0) == t % SUBLANES
        for b in range(nb):
            for g in range(groups):
                lo = (b * groups + g) * SUBLANES
                yrow = jnp.sum(jnp.where(eye, y_all[lo:lo + SUBLANES, :], 0.0), axis=0, keepdims=True)
                cur = y_ref[b, pl.ds(t8, SUBLANES), g * LANES:(g + 1) * LANES]
                y_ref[b, pl.ds(t8, SUBLANES), g * LANES:(g + 1) * LANES] = jnp.where(
                    mine, jnp.broadcast_to(yrow, (SUBLANES, LANES)), cur)
        return 0

    y_ref[...] = jnp.zeros_like(y_ref)
    lax.fori_loop(0, tc, step, 0)

    @pl.when(tt == pl.num_programs(1) - 1)
    def _():
        sout_ref[...] = s_s[...]


def _rwkv_scan(vecs, s0, gmat, batch, seq, row0, nb, tc):
    nt = seq // tc
    blk0 = row0 // tc

    def tok(n):
        return pl.BlockSpec((tc, D_MODEL), lambda bb, tt: (blk0 + (bb * nb + n) * nt + tt, 0))

    in_specs = [tok(n) for _ in range(6) for n in range(nb)]
    args = [a for a in vecs for _ in range(nb)]
    state = pl.BlockSpec((nb, HEAD_DIM, D_MODEL), lambda bb, tt: (bb, 0, 0))
    in_specs += [state, pl.BlockSpec((LANES, LANES), lambda bb, tt: (0, 0))]
    return pl.pallas_call(
        functools.partial(_rwkv_scan_kernel, nb=nb, tc=tc), grid=(batch // nb, nt), in_specs=in_specs,
        out_specs=[pl.BlockSpec((nb, tc, D_MODEL), lambda bb, tt: (bb, tt, 0)), state],
        out_shape=[jax.ShapeDtypeStruct((batch, seq, D_MODEL), F32),
                   jax.ShapeDtypeStruct((batch, HEAD_DIM, D_MODEL), F32)],
        scratch_shapes=[pltpu.VMEM((nb, HEAD_DIM, D_MODEL), F32)],
        compiler_params=_params("parallel", "arbitrary"), name="rwkv_scan")(*args, s0, gmat)


def _rwkv_out_kernel(y_ref, r_ref, k_ref, v_ref, g_ref, x_ref, vec_ref, gmat_ref, wo_ref, gpost_ref, o_ref):
    r_k, ln_w, ln_b = (vec_ref[i:i + 1, :] for i in range(3))
    gmat = gmat_ref[...]
    y = y_ref[...]
    mean = _split_dot(_tile_sum(y), gmat, 3) * (1.0 / HEAD_DIM)
    yc = y - _tile8(mean)
    var = _split_dot(_tile_sum(yc * yc), gmat, 3) * (1.0 / HEAD_DIM)
    yn = yc * _tile8(lax.rsqrt(var + GN_EPS)) * ln_w + ln_b
    bonus = _split_dot(_tile_sum(r_ref[...] * k_ref[...] * r_k), gmat, 3)
    out = (yn + _tile8(bonus) * v_ref[...]) * g_ref[...]
    m = _dot(out.astype(BF16), wo_ref[...])
    o_ref[...] = x_ref[...] + _rms(m, gpost_ref[...])


def _rwkv_out(y, r, k, v, g, x, vecs, gmat, wo, gpost, tm=256):
    t = x.shape[0]
    row = lambda i: (i, 0)
    fixed = lambda i: (0, 0)
    tok = pl.BlockSpec((tm, D_MODEL), row)
    consts = [vecs, gmat, wo, gpost]
    return pl.pallas_call(
        _rwkv_out_kernel, grid=(t // tm,), in_specs=[tok] * 6 + [pl.BlockSpec(a.shape, fixed) for a in consts],
        out_specs=tok, out_shape=jax.ShapeDtypeStruct((t, D_MODEL), F32),
        compiler_params=_params("parallel"), name="rwkv_out")(y, r, k, v, g, x, *consts)


def _block_diag_q(q_perm, db, ds):
    inv = np.argsort(np.asarray(HEAD_PERM))
    q = q_perm.reshape(db, ds, N_HEADS, HEAD_DIM)[:, :, inv]
    q = q.reshape(db, ds, KV_HEADS, N_HEADS // KV_HEADS, HEAD_DIM).transpose(0, 2, 3, 1, 4)
    eye = jnp.eye(KV_HEADS, dtype=q.dtype)
    qbd = q[:, :, :, :, None, :] * eye[None, :, None, None, :, None]
    return qbd.reshape(db, N_HEADS * ds, D_KV)


def _undo_block_diag(o, db, ds):
    o = o.reshape(db, KV_HEADS, N_HEADS // KV_HEADS, ds, KV_HEADS, HEAD_DIM)
    o = jnp.stack([o[:, kv, :, :, kv, :] for kv in range(KV_HEADS)], axis=1)
    o = o.transpose(0, 3, 1, 2, 4).reshape(db, ds, N_HEADS, HEAD_DIM)
    return o[:, :, np.asarray(HEAD_PERM)].reshape(db * ds, D_MODEL)


def _pad_page(x, db, ds):
    return jnp.pad(x.reshape(db, ds, -1), ((0, 0), (0, PAGE - ds), (0, 0)))


def _rope_tables(pos):
    inv_freq = ROPE_THETA ** (-jnp.arange(0, ROT_DIM, 2, dtype=F32) / ROT_DIM)
    ang = pos.astype(F32)[:, None] * inv_freq[None, :]
    cos, sin = jnp.cos(ang), jnp.sin(ang)
    half = ROT_DIM // 2
    ones = jnp.ones((pos.shape[0], HEAD_DIM - ROT_DIM), F32)
    zeros = jnp.zeros((pos.shape[0], HEAD_DIM - ROT_DIM), F32)
    zh = jnp.zeros_like(sin)
    a = jnp.concatenate([cos, cos, ones], axis=1)
    m = jnp.concatenate([-sin, zh, zeros], axis=1)
    p = jnp.concatenate([zh, sin, zeros], axis=1)
    rep = LANES // HEAD_DIM
    return tuple(jnp.tile(z, (1, rep)) for z in (a, m, p))


def kernel(x_prompt, x_sample, cache_fox_k_l0, cache_fox_v_l0, cache_fox_logf_l0, state_rwkv_wkv_l1, state_rwkv_shift_l1, cache_swa_k_l2, cache_swa_v_l2, cache_fox_k_l3, cache_fox_v_l3, cache_fox_logf_l3, page_table, p_prompt, p_sample, norm_mix_pre, norm_mix_post, norm_ffn_pre, norm_ffn_post, ffn_w_up, ffn_w_down, ple_w_proj, ple_w_gate, fox_w_q, fox_w_k, fox_w_v, fox_w_f, fox_b_f, fox_w_o, rwkv_mu, rwkv_w_r, rwkv_w_k, rwkv_w_v, rwkv_w_o, rwkv_w0, rwkv_w1, rwkv_w2, rwkv_a0, rwkv_a1, rwkv_a2, rwkv_g1, rwkv_g2, rwkv_k_k, rwkv_k_a, rwkv_r_k, rwkv_ln_w, rwkv_ln_b, swa_w_q, swa_b_q, swa_w_k, swa_b_k, swa_w_v, swa_b_v, swa_sinks, swa_w_o):
    bp, sp, _ = x_prompt.shape
    db, ds, _ = x_sample.shape
    depth = norm_mix_pre.shape[0]
    tp, ts = bp * sp, db * ds
    n_pages = page_table.shape[1]
    past = n_pages * PAGE
    bf = lambda a: a.astype(BF16)
    vec = lambda a: a.reshape(1, -1).astype(F32)

    x = jnp.concatenate([x_prompt.reshape(tp, D_MODEL), x_sample.reshape(ts, D_MODEL)], axis=0)
    p_all = jnp.concatenate([p_prompt.reshape(depth, tp, D_PLE), p_sample.reshape(depth, ts, D_PLE)], axis=1)
    pos = jnp.concatenate([jnp.tile(jnp.arange(sp), bp), jnp.tile(past + jnp.arange(ds), db)])
    gmat = (np.arange(LANES)[:, None] % N_HEADS == np.arange(LANES)[None, :] % N_HEADS)
    gmat = jnp.asarray(gmat, BF16)
    fox_caches = ((cache_fox_k_l0, cache_fox_v_l0, cache_fox_logf_l0), (cache_fox_k_l3, cache_fox_v_l3, cache_fox_logf_l3))
    new_state = []

    for i in range(depth):
        kind, j = i % N_MIXERS, i // N_MIXERS
        g_pre = vec(norm_mix_pre[i])
        g_post = vec(norm_mix_post[i])
        if kind == 0:
            wf = jnp.pad(fox_w_f[j], ((0, 0), (0, LANES - N_HEADS)))
            w = bf(jnp.concatenate([fox_w_q[j][:, Q_COLS], fox_w_k[j], fox_w_v[j], wf], axis=1))
            b = jnp.concatenate([jnp.zeros((D_MODEL + 2 * D_KV,), F32), fox_b_f[j], jnp.zeros((LANES - N_HEADS,), F32)])
            q, k, v, lf = _proj(x, g_pre, w, vec(b), None)
            lf = lf[:, :N_HEADS]
            lf_p = lf[:tp].reshape(bp, sp, N_HEADS)
            ck = _cumsum(lf_p.transpose(0, 2, 1))
            cq = ck.transpose(0, 2, 1).reshape(tp, N_HEADS)
            o_p = _flash(q, k, v, cq, ck, bp, sp)
            ck_cache, cv_cache, clf_cache = fox_caches[j]
            n_pool = ck_cache.shape[0]
            pages_t = lambda c: c.transpose(0, 2, 3, 1).reshape(n_pool, D_KV, PAGE)
            new_t = lambda z: _pad_page(z, db, ds).transpose(0, 2, 1)
            o_s = _fox_sample(
                page_table, _block_diag_q(q[tp:], db, ds), new_t(k[tp:]), new_t(v[tp:]), new_t(lf[tp:]),
                pages_t(ck_cache), pages_t(cv_cache), clf_cache.transpose(0, 2, 1))
            o = jnp.concatenate([o_p, _undo_block_diag(o_s, db, ds)], axis=0)
            x = _oproj(o, bf(fox_w_o[j][Q_COLS, :]), g_post, x)
            new_state.append((k[:tp].reshape(bp, sp, KV_HEADS, HEAD_DIM), v[:tp].reshape(bp, sp, KV_HEADS, HEAD_DIM),
                              lf_p, k[tp:].reshape(db, ds, KV_HEADS, HEAD_DIM),
                              v[tp:].reshape(db, ds, KV_HEADS, HEAD_DIM), lf[tp:].reshape(db, ds, N_HEADS)))
        elif kind == 1:
            h = _norm(x, g_pre)
            h_p = h[:tp].reshape(bp, sp, D_MODEL)
            h_s = h[tp:].reshape(db, ds, D_MODEL)
            hp = jnp.concatenate([
                jnp.concatenate([jnp.zeros((bp, 1, D_MODEL), F32), h_p[:, :-1]], axis=1).reshape(tp, D_MODEL),
                jnp.concatenate([state_rwkv_shift_l1[:, None, :], h_s[:, :-1]], axis=1).reshape(ts, D_MODEL)], axis=0)
            pc = lambda a: a[:, R_COLS]
            vecs = jnp.stack([rwkv_w0[j][R_COLS], rwkv_a0[j][R_COLS], rwkv_k_k[j][R_COLS], rwkv_k_a[j][R_COLS]])
            r, w, k, v, kk, kka, g = _rwkv_proj(
                h, hp, rwkv_mu[j], bf(pc(rwkv_w_r[j])), bf(pc(rwkv_w_k[j])), bf(pc(rwkv_w_v[j])), bf(rwkv_w1[j]),
                bf(pc(rwkv_w2[j])), bf(rwkv_a1[j]), bf(pc(rwkv_a2[j])), bf(rwkv_g1[j]), bf(pc(rwkv_g2[j])), vecs, gmat)
            scan_in = (r, w, k, v, kk, kka)
            to_lanes = lambda s: s.transpose(0, 2, 3, 1).reshape(s.shape[0], HEAD_DIM, D_MODEL)
            from_lanes = lambda s: s.reshape(s.shape[0], HEAD_DIM, HEAD_DIM, N_HEADS).transpose(0, 3, 1, 2)
            y_p, s_p = _rwkv_scan(scan_in, jnp.zeros((bp, HEAD_DIM, D_MODEL), F32), gmat, bp, sp, 0, bp, 64)
            y_s, s_s = _rwkv_scan(scan_in, to_lanes(state_rwkv_wkv_l1.astype(F32)), gmat, db, ds, tp, 4, ds)
            y = jnp.concatenate([y_p.reshape(tp, D_MODEL), y_s.reshape(ts, D_MODEL)], axis=0)
            ovecs = jnp.stack([rwkv_r_k[j].reshape(-1)[R_COLS], rwkv_ln_w[j][R_COLS], rwkv_ln_b[j][R_COLS]])
            x = _rwkv_out(y, r, k, v, g, x, ovecs, gmat, bf(rwkv_w_o[j][R_COLS, :]), g_post)
            new_state.append((from_lanes(s_p), h_p[:, -1], from_lanes(s_s), h_s[:, -1]))
        else:
            w = bf(jnp.concatenate([swa_w_q[j][:, Q_COLS], swa_w_k[j], swa_w_v[j]], axis=1))
            b = jnp.concatenate([swa_b_q[j][Q_COLS], swa_b_k[j], swa_b_v[j]])
            q, k, v = _proj(x, g_pre, w, vec(b), _rope_tables(pos))
            o_p = _swa_band(q, k, v, vec(swa_sinks[j]), bp, sp)
            keep = cache_swa_k_l2.shape[1]
            k_s = k[tp:].reshape(db, ds, D_KV)
            v_s = v[tp:].reshape(db, ds, D_KV)
            kbuf = cache_swa_k_l2.reshape(db, keep, D_KV)
            vbuf = cache_swa_v_l2.reshape(db, keep, D_KV)
            sink_rows = jnp.repeat(swa_sinks[j].astype(F32), ds).reshape(N_HEADS * ds, 1)
            o_s = _swa_sample(_block_diag_q(q[tp:], db, ds), kbuf, vbuf, _pad_page(k[tp:], db, ds),
                              _pad_page(v[tp:], db, ds), sink_rows)
            o = jnp.concatenate([o_p, _undo_block_diag(o_s, db, ds)], axis=0)
            x = _oproj(o, bf(swa_w_o[j][Q_COLS, :]), g_post, x)
            k_p = k[:tp].reshape(bp, sp, KV_HEADS, HEAD_DIM)
            v_p = v[:tp].reshape(bp, sp, KV_HEADS, HEAD_DIM)
            wk = min(WINDOW, sp)
            new_state.append((k_p[:, sp - wk:], v_p[:, sp - wk:],
                              jnp.concatenate([kbuf, k_s], axis=1)[:, ds:].reshape(db, keep, KV_HEADS, HEAD_DIM),
                              jnp.concatenate([vbuf, v_s], axis=1)[:, ds:].reshape(db, keep, KV_HEADS, HEAD_DIM)))
        x = _ffn(x, vec(norm_ffn_pre[i]), bf(ffn_w_up[i]), bf(ffn_w_down[i]), vec(norm_ffn_post[i]), p_all[i],
                 bf(ple_w_proj[i]), bf(ple_w_gate[i]))

    outs = [x[:tp].reshape(bp, sp, D_MODEL), x[tp:].reshape(db, ds, D_MODEL)]
    for st in new_state:
        outs.extend(st)
    return tuple(outs)
```

```python
import functools

import numpy as np
import jax
import jax.numpy as jnp
from jax import lax
from jax.experimental import pallas as pl
from jax.experimental.pallas import tpu as pltpu

F32 = jnp.float32
BF16 = jnp.bfloat16

D_MODEL = 1024
HEAD_DIM = 64
N_HEADS = 16
KV_HEADS = 4
D_KV = KV_HEADS * HEAD_DIM
D_FF = 4096
D_PLE = 256
PAGE = 128
WINDOW = 128
ROT_DIM = 16
ROPE_THETA = 500000.0
NORM_EPS = 1e-6
GN_EPS = 64e-5
N_MIXERS = 3
LANES = 128
SUBLANES = 8
NEG = -1e30
VMEM_LIMIT = 48 * 1024 * 1024
SCALE = HEAD_DIM ** -0.5

HEAD_PERM = (0, 4, 1, 5, 2, 6, 3, 7, 8, 12, 9, 13, 10, 14, 11, 15)
Q_COLS = np.concatenate([np.arange(HEAD_DIM) + HEAD_DIM * h for h in HEAD_PERM])
R_COLS = (np.arange(D_MODEL) % N_HEADS) * HEAD_DIM + np.arange(D_MODEL) // N_HEADS


def _params(*sem):
    return pltpu.CompilerParams(dimension_semantics=sem, vmem_limit_bytes=VMEM_LIMIT)


def _dot(a, b):
    return jnp.dot(a, b, preferred_element_type=F32)


def _dot_nt(a, b):
    return lax.dot_general(a, b, (((1,), (1,)), ((), ())), preferred_element_type=F32)


def _rms(x, g):
    return x * lax.rsqrt(jnp.mean(x * x, axis=-1, keepdims=True) + NORM_EPS) * g


def _log_sigmoid(x):
    return jnp.minimum(x, 0.0) - jnp.log1p(jnp.exp(-jnp.abs(x)))


def _softplus(x):
    return jnp.maximum(x, 0.0) + jnp.log1p(jnp.exp(-jnp.abs(x)))


def _split_dot(x, g, parts):
    out = None
    for _ in range(parts):
        piece = x.astype(BF16)
        term = _dot(piece, g)
        out = term if out is None else out + term
        x = x - piece.astype(F32)
    return out


def _tile_sum(x):
    out = x[:, 0:LANES]
    for t in range(1, D_MODEL // LANES):
        out = out + x[:, t * LANES:(t + 1) * LANES]
    return out


def _tile8(x):
    return jnp.concatenate([x] * (D_MODEL // LANES), axis=1)


def _rope(x, a, m, p):
    chunks = []
    for c in range(x.shape[1] // LANES):
        xc = x[:, c * LANES:(c + 1) * LANES]
        chunks.append(xc * a + pltpu.roll(xc, LANES - ROT_DIM // 2, 1) * m + pltpu.roll(xc, ROT_DIM // 2, 1) * p)
    return jnp.concatenate(chunks, axis=1)


def _proj_kernel(x_ref, g_ref, w_ref, b_ref, *rest, rope):
    if rope:
        ra_ref, rm_ref, rp_ref, q_ref, k_ref, v_ref = rest
    else:
        q_ref, k_ref, v_ref, lf_ref = rest
    h = _rms(x_ref[...], g_ref[...]).astype(BF16)
    res = _dot(h, w_ref[...]) + b_ref[...]
    q = res[:, :D_MODEL]
    k = res[:, D_MODEL:D_MODEL + D_KV]
    v = res[:, D_MODEL + D_KV:D_MODEL + 2 * D_KV]
    if rope:
        a, m, p = ra_ref[...], rm_ref[...], rp_ref[...]
        q = _rope(q, a, m, p)
        k = _rope(k, a, m, p)
    else:
        lf_ref[...] = _log_sigmoid(res[:, D_MODEL + 2 * D_KV:])
    q_ref[...] = (q * SCALE).astype(BF16)
    k_ref[...] = k
    v_ref[...] = v


def _proj(x, g, w, b, rope_tabs, tm=512):
    t = x.shape[0]
    n = w.shape[1]
    rope = rope_tabs is not None
    row = lambda i: (i, 0)
    fixed = lambda i: (0, 0)
    in_specs = [pl.BlockSpec((tm, D_MODEL), row), pl.BlockSpec((1, D_MODEL), fixed),
                pl.BlockSpec((D_MODEL, n), fixed), pl.BlockSpec((1, n), fixed)]
    args = [x, g, w, b]
    out_shape = [jax.ShapeDtypeStruct((t, D_MODEL), BF16), jax.ShapeDtypeStruct((t, D_KV), F32),
                 jax.ShapeDtypeStruct((t, D_KV), F32)]
    out_specs = [pl.BlockSpec((tm, D_MODEL), row), pl.BlockSpec((tm, D_KV), row), pl.BlockSpec((tm, D_KV), row)]
    if rope:
        in_specs += [pl.BlockSpec((tm, LANES), row)] * 3
        args += list(rope_tabs)
    else:
        out_shape.append(jax.ShapeDtypeStruct((t, LANES), F32))
        out_specs.append(pl.BlockSpec((tm, LANES), row))
    return pl.pallas_call(
        functools.partial(_proj_kernel, rope=rope), grid=(t // tm,), in_specs=in_specs, out_specs=out_specs,
        out_shape=out_shape, compiler_params=_params("parallel"),
        name="swa_proj" if rope else "fox_proj")(*args)


def _oproj_kernel(op_ref, os_ref, w_ref, g_ref, x_ref, y_ref, *, prompt_tiles):
    o = jnp.where(pl.program_id(0) < prompt_tiles, op_ref[...], os_ref[...])
    m = _dot(o, w_ref[...])
    y_ref[...] = x_ref[...] + _rms(m, g_ref[...])


def _oproj(o_p, o_s, w, g, x, tm=512):
    t = x.shape[0]
    n_p = o_p.shape[0] // tm
    row = lambda i: (i, 0)
    fixed = lambda i: (0, 0)
    return pl.pallas_call(
        functools.partial(_oproj_kernel, prompt_tiles=n_p), grid=(t // tm,),
        in_specs=[pl.BlockSpec((tm, D_MODEL), lambda i: (jnp.minimum(i, n_p - 1), 0)),
                  pl.BlockSpec((tm, D_MODEL), lambda i: (jnp.maximum(i - n_p, 0), 0)),
                  pl.BlockSpec((D_MODEL, D_MODEL), fixed), pl.BlockSpec((1, D_MODEL), fixed),
                  pl.BlockSpec((tm, D_MODEL), row)],
        out_specs=pl.BlockSpec((tm, D_MODEL), row), out_shape=jax.ShapeDtypeStruct((t, D_MODEL), F32),
        compiler_params=_params("parallel"), name="out_proj")(o_p, o_s, w, g, x)


def _ffn_kernel(x_ref, gpre_ref, wup_ref, wdn_ref, gpost_ref, pp_ref, ps_ref, wp_ref, wg_ref, y_ref, h_s, acc_s, *,
                prompt_tiles):
    kf = pl.program_id(1)

    @pl.when(kf == 0)
    def _():
        h_s[...] = _rms(x_ref[...], gpre_ref[...]).astype(BF16)
        acc_s[...] = jnp.zeros_like(acc_s)

    u = jnp.maximum(_dot(h_s[...], wup_ref[...]), 0.0)
    acc_s[...] += _dot((u * u).astype(BF16), wdn_ref[...])

    @pl.when(kf == pl.num_programs(1) - 1)
    def _():
        x1 = x_ref[...] + _rms(acc_s[...], gpost_ref[...])
        gate = jax.nn.sigmoid(_dot(x1.astype(BF16), wg_ref[...]))
        p = jnp.where(pl.program_id(0) < prompt_tiles, pp_ref[...], ps_ref[...])
        y_ref[...] = x1 + _dot(p.astype(BF16), wp_ref[...]) * gate


def _ffn(x, gpre, wup, wdn, gpost, p_prompt, p_sample, layer, tp, wp, wg, tm=1024, tf=512):
    t = x.shape[0]
    n_p = tp // tm
    n_s = (t - tp) // tm
    row = lambda i, k: (i, 0)
    fixed = lambda i, k: (0, 0)
    return pl.pallas_call(
        functools.partial(_ffn_kernel, prompt_tiles=n_p), grid=(t // tm, D_FF // tf),
        in_specs=[pl.BlockSpec((tm, D_MODEL), row), pl.BlockSpec((1, D_MODEL), fixed),
                  pl.BlockSpec((D_MODEL, tf), lambda i, k: (0, k)), pl.BlockSpec((tf, D_MODEL), lambda i, k: (k, 0)),
                  pl.BlockSpec((1, D_MODEL), fixed),
                  pl.BlockSpec((tm, D_PLE), lambda i, k: (layer * n_p + jnp.minimum(i, n_p - 1), 0)),
                  pl.BlockSpec((tm, D_PLE), lambda i, k: (layer * n_s + jnp.maximum(i - n_p, 0), 0)),
                  pl.BlockSpec((D_PLE, D_MODEL), fixed), pl.BlockSpec((D_MODEL, D_MODEL), fixed)],
        out_specs=pl.BlockSpec((tm, D_MODEL), row), out_shape=jax.ShapeDtypeStruct((t, D_MODEL), F32),
        scratch_shapes=[pltpu.VMEM((tm, D_MODEL), BF16), pltpu.VMEM((tm, D_MODEL), F32)],
        compiler_params=_params("parallel", "arbitrary"), name="ffn_ple")(
            x, gpre, wup, wdn, gpost, p_prompt, p_sample, wp, wg)


def _cumsum_kernel(x_ref, c_ref):
    n = x_ref.shape[1]
    lane = lax.broadcasted_iota(jnp.int32, (N_HEADS, LANES), 1)
    carry = jnp.zeros((N_HEADS, 1), F32)
    for c in range(n // LANES):
        blk = x_ref[:, c * LANES:(c + 1) * LANES]
        sh = 1
        while sh < LANES:
            blk = blk + jnp.where(lane >= sh, pltpu.roll(blk, sh, 1), 0.0)
            sh *= 2
        blk = blk + carry
        c_ref[:, c * LANES:(c + 1) * LANES] = blk
        carry = blk[:, LANES - 1:LANES]


def _cumsum(x):
    b, h, t = x.shape
    spec = pl.BlockSpec((None, h, t), lambda i: (i, 0, 0))
    return pl.pallas_call(_cumsum_kernel, grid=(b,), in_specs=[spec], out_specs=spec,
                          out_shape=jax.ShapeDtypeStruct(x.shape, F32), compiler_params=_params("parallel"),
                          name="logf_cumsum")(x)


def _flash_kernel(qi_tab, ki_tab, first_tab, last_tab, q_ref, k_ref, v_ref, cq_ref, ck_ref, o_ref, qm_s, m_s, l_s, acc_s,
                  *, tq, tk):
    step = pl.program_id(1)
    qi = qi_tab[step]
    ki = ki_tab[step]
    lane = lax.broadcasted_iota(jnp.int32, (tq, LANES), 1)
    rep = tk // LANES

    @pl.when(first_tab[step] == 1)
    def _():
        m_s[...] = jnp.full_like(m_s, NEG)
        l_s[...] = jnp.zeros_like(l_s)
        acc_s[...] = jnp.zeros_like(acc_s)
        for pp in range(N_HEADS):
            qp = q_ref[:, (pp // 2) * LANES:(pp // 2 + 1) * LANES]
            keep = (lane < HEAD_DIM) if pp % 2 == 0 else (lane >= HEAD_DIM)
            qm_s[pp * tq:(pp + 1) * tq, :] = jnp.where(keep, qp, jnp.zeros_like(qp))

    kb = k_ref[...].astype(BF16)
    vb = v_ref[...].astype(BF16)
    row = qi * tq + lax.broadcasted_iota(jnp.int32, (tq, tk), 0)
    col = ki * tk + lax.broadcasted_iota(jnp.int32, (tq, tk), 1)
    mask = col <= row
    group = N_HEADS // 2
    for g0 in range(0, N_HEADS, group):
        kv = g0 // group
        kp = kb[:, kv * LANES:(kv + 1) * LANES]
        vp = vb[:, kv * LANES:(kv + 1) * LANES]
        rows = slice(g0 * tq, (g0 + group) * tq)
        s = _dot_nt(qm_s[rows, :], kp)
        parts = []
        for i in range(group):
            h = HEAD_PERM[g0 + i]
            sh = s[i * tq:(i + 1) * tq, :] - ck_ref[h:h + 1, :]
            parts.append(jnp.where(mask, sh + cq_ref[:, h:h + 1], NEG))
        s = jnp.concatenate(parts, axis=0)
        m_old = m_s[rows, :]
        m_new = jnp.maximum(m_old, jnp.max(s, axis=1, keepdims=True))
        alpha = jnp.exp(m_old - m_new)
        p = jnp.exp(s - jnp.concatenate([m_new] * rep, axis=1))
        l_s[rows, :] = alpha * l_s[rows, :] + jnp.sum(p, axis=1, keepdims=True)
        m_s[rows, :] = m_new
        pv = _dot(p.astype(BF16), vp)
        for i in range(0, group, 2):
            pair = (g0 + i) // 2
            a = acc_s[:, pair * LANES:(pair + 1) * LANES]
            a0, a1 = alpha[i * tq:(i + 1) * tq, :], alpha[(i + 1) * tq:(i + 2) * tq, :]
            acc_s[:, pair * LANES:(pair + 1) * LANES] = jnp.where(
                lane < HEAD_DIM, a0 * a + pv[i * tq:(i + 1) * tq, :], a1 * a + pv[(i + 1) * tq:(i + 2) * tq, :])

    @pl.when(last_tab[step] == 1)
    def _():
        for pair in range(N_HEADS // 2):
            l0 = l_s[(2 * pair) * tq:(2 * pair + 1) * tq, :]
            l1 = l_s[(2 * pair + 1) * tq:(2 * pair + 2) * tq, :]
            inv = jnp.where(lane < HEAD_DIM, 1.0 / l0, 1.0 / l1)
            o_ref[:, pair * LANES:(pair + 1) * LANES] = (acc_s[:, pair * LANES:(pair + 1) * LANES] * inv).astype(BF16)


def _flash(q, k, v, cq, ck, batch, seq, tq=512):
    tk = tq
    nq = seq // tq
    pairs = [(a, b) for a in range(nq) for b in range(a + 1)]
    qi_tab = jnp.asarray([a for a, _ in pairs], jnp.int32)
    ki_tab = jnp.asarray([b for _, b in pairs], jnp.int32)
    first_tab = jnp.asarray([int(b == 0) for a, b in pairs], jnp.int32)
    last_tab = jnp.asarray([int(b == a) for a, b in pairs], jnp.int32)
    qmap = lambda b, s, qt, kt, ft, lt: (b * nq + qt[s], 0)
    kmap = lambda b, s, qt, kt, ft, lt: (b * nq + kt[s], 0)
    in_specs = [pl.BlockSpec((tq, D_MODEL), qmap), pl.BlockSpec((tk, D_KV), kmap), pl.BlockSpec((tk, D_KV), kmap),
                pl.BlockSpec((tq, N_HEADS), qmap),
                pl.BlockSpec((None, N_HEADS, tk), lambda b, s, qt, kt, ft, lt: (b, 0, kt[s]))]
    grid_spec = pltpu.PrefetchScalarGridSpec(
        num_scalar_prefetch=4, grid=(batch, len(pairs)), in_specs=in_specs,
        out_specs=pl.BlockSpec((tq, D_MODEL), qmap),
        scratch_shapes=[pltpu.VMEM((N_HEADS * tq, LANES), BF16), pltpu.VMEM((N_HEADS * tq, LANES), F32),
                        pltpu.VMEM((N_HEADS * tq, LANES), F32), pltpu.VMEM((tq, D_MODEL), F32)])
    return pl.pallas_call(
        functools.partial(_flash_kernel, tq=tq, tk=tk), grid_spec=grid_spec,
        out_shape=jax.ShapeDtypeStruct((batch * seq, D_MODEL), BF16),
        compiler_params=_params("parallel", "arbitrary"),
        name="fox_prompt_attn")(qi_tab, ki_tab, first_tab, last_tab, q, k, v, cq, ck)


def _swa_band_kernel(q_ref, kp_ref, ko_ref, vp_ref, vo_ref, sink_ref, o_ref, *, tq):
    qi = pl.program_id(1)
    nk = WINDOW + tq
    kb = jnp.concatenate([kp_ref[...], ko_ref[...]], axis=0).astype(BF16)
    vb = jnp.concatenate([vp_ref[...], vo_ref[...]], axis=0).astype(BF16)
    lane = lax.broadcasted_iota(jnp.int32, (tq, LANES), 1)
    a = lax.broadcasted_iota(jnp.int32, (tq, nk), 0)
    c = lax.broadcasted_iota(jnp.int32, (tq, nk), 1)
    mask = (c > a) & (c <= a + WINDOW) & (qi * tq + c >= WINDOW)
    for pair in range(N_HEADS // 2):
        kv = pair // 4
        kp = kb[:, kv * LANES:(kv + 1) * LANES]
        vp = vb[:, kv * LANES:(kv + 1) * LANES]
        qp = q_ref[:, pair * LANES:(pair + 1) * LANES]
        outs = []
        for half in range(2):
            h = HEAD_PERM[2 * pair + half]
            keep = (lane < HEAD_DIM) if half == 0 else (lane >= HEAD_DIM)
            s = _dot_nt(jnp.where(keep, qp, jnp.zeros_like(qp)), kp)
            s = jnp.where(mask, s, NEG)
            sink = sink_ref[:, h:h + 1]
            m = jnp.maximum(jnp.max(s, axis=1, keepdims=True), sink)
            p = jnp.exp(s - m)
            l = jnp.sum(p, axis=1, keepdims=True) + jnp.exp(sink - m)
            outs.append(_dot(p.astype(BF16), vp) * (1.0 / l))
        o_ref[:, pair * LANES:(pair + 1) * LANES] = jnp.where(lane < HEAD_DIM, outs[0], outs[1]).astype(BF16)


def _swa_band(q, k, v, sinks, batch, seq, tq=256):
    nq = seq // tq
    r = tq // WINDOW
    qmap = lambda b, i: (b * nq + i, 0)
    pmap = lambda b, i: (jnp.maximum((b * nq + i) * r - 1, 0), 0)
    return pl.pallas_call(
        functools.partial(_swa_band_kernel, tq=tq), grid=(batch, nq),
        in_specs=[pl.BlockSpec((tq, D_MODEL), qmap), pl.BlockSpec((WINDOW, D_KV), pmap), pl.BlockSpec((tq, D_KV), qmap),
                  pl.BlockSpec((WINDOW, D_KV), pmap), pl.BlockSpec((tq, D_KV), qmap),
                  pl.BlockSpec((1, N_HEADS), lambda b, i: (0, 0))],
        out_specs=pl.BlockSpec((tq, D_MODEL), qmap), out_shape=jax.ShapeDtypeStruct((batch * seq, D_MODEL), BF16),
        compiler_params=_params("parallel", "parallel"), name="swa_prompt_attn")(q, k, k, v, v, sinks)


def _expand_heads(x16):
    return jnp.concatenate([jnp.broadcast_to(x16[h:h + 1, :], (SUBLANES, LANES)) for h in range(N_HEADS)], axis=0)


def _decode_update(qbd, k, v, bias, mask, m_s, l_s, acc_s):
    s = _dot_nt(qbd, k.astype(BF16))
    if bias is not None:
        s = s + bias
    if mask is not None:
        s = jnp.where(mask, s, NEG)
    m_old = m_s[...]
    m_new = jnp.maximum(m_old, jnp.max(s, axis=1, keepdims=True))
    alpha = jnp.exp(m_old - m_new)
    p = jnp.exp(s - m_new)
    l_s[...] = alpha * l_s[...] + jnp.sum(p, axis=1, keepdims=True)
    m_s[...] = m_new
    acc_s[...] = alpha * acc_s[...] + _dot(p.astype(BF16), v.astype(BF16))


def _suffix_sums(lf):
    lane = lax.broadcasted_iota(jnp.int32, lf.shape, 1)
    x = lf
    sh = 1
    while sh < LANES:
        x = x + jnp.where(lane < LANES - sh, pltpu.roll(x, LANES - sh, 1), 0.0)
        sh *= 2
    return x


def _fox_sample_kernel(pt_ref, q_ref, knew_ref, vnew_ref, lfnew_ref, *rest, pages_per_step):
    n = pages_per_step
    k_refs, v_refs, lf_refs = rest[:n], rest[n:2 * n], rest[2 * n:3 * n]
    o_ref, m_s, l_s, acc_s, carry_s, rowc_s = rest[3 * n:]
    j = pl.program_id(1)
    row_q = lax.broadcasted_iota(jnp.int32, (LANES, LANES), 0) % SUBLANES
    lane = lax.broadcasted_iota(jnp.int32, (LANES, LANES), 1)

    def update(kt, vt, bias, mask):
        s = _dot(q_ref[...], kt) + bias
        if mask is not None:
            s = jnp.where(mask, s, NEG)
        m_old = m_s[...]
        m_new = jnp.maximum(m_old, jnp.max(s, axis=1, keepdims=True))
        alpha = jnp.exp(m_old - m_new)
        p = jnp.exp(s - m_new)
        l_s[...] = alpha * l_s[...] + jnp.sum(p, axis=1, keepdims=True)
        m_s[...] = m_new
        acc_s[...] = alpha * acc_s[...] + _dot_nt(p.astype(BF16), vt)

    @pl.when(j == 0)
    def _():
        m_s[...] = jnp.full_like(m_s, NEG)
        l_s[...] = jnp.zeros_like(l_s)
        acc_s[...] = jnp.zeros_like(acc_s)
        lf = lfnew_ref[...]
        inc = _suffix_sums(lf)
        bias = _expand_heads(inc - lf)
        rowc = -jnp.sum(jnp.where(lane == row_q, bias, 0.0), axis=1, keepdims=True)
        rowc_s[...] = rowc
        carry_s[...] = inc[:, 0:1]
        update(knew_ref[...].astype(BF16), vnew_ref[...].astype(BF16), bias + rowc, lane <= row_q)

    carry = carry_s[...]
    biases = []
    for i in range(n):
        lf = lf_refs[i][...]
        inc = _suffix_sums(lf)
        biases.append(_expand_heads(inc - lf + carry))
        carry = carry + inc[:, 0:1]
    carry_s[...] = carry
    kt = jnp.concatenate([r[...].astype(BF16) for r in k_refs], axis=1)
    vt = jnp.concatenate([r[...].astype(BF16) for r in v_refs], axis=1)
    update(kt, vt, jnp.concatenate(biases, axis=1) + rowc_s[...], None)

    @pl.when(j == pl.num_programs(1) - 1)
    def _():
        o_ref[...] = (acc_s[...] * (1.0 / l_s[...])).astype(BF16)


def _fox_sample(page_table, qbd, knew_t, vnew_t, lfnew, cache_kt, cache_vt, cache_lf_t, pages_per_step=32):
    db, n_pages = page_table.shape
    n = pages_per_step
    seq = lambda b, j, pt: (b, 0, 0)

    def page(i):
        return lambda b, j, pt: (pt[b, n_pages - 1 - (n * j + i)], 0, 0)

    in_specs = [pl.BlockSpec((None, LANES, D_KV), seq), pl.BlockSpec((None, D_KV, PAGE), seq),
                pl.BlockSpec((None, D_KV, PAGE), seq), pl.BlockSpec((None, N_HEADS, PAGE), seq)]
    in_specs += [pl.BlockSpec((None, D_KV, PAGE), page(i)) for i in range(n)]
    in_specs += [pl.BlockSpec((None, D_KV, PAGE), page(i)) for i in range(n)]
    in_specs += [pl.BlockSpec((None, N_HEADS, PAGE), page(i)) for i in range(n)]
    grid_spec = pltpu.PrefetchScalarGridSpec(
        num_scalar_prefetch=1, grid=(db, n_pages // n), in_specs=in_specs,
        out_specs=pl.BlockSpec((None, LANES, D_KV), seq),
        scratch_shapes=[pltpu.VMEM((LANES, 1), F32), pltpu.VMEM((LANES, 1), F32), pltpu.VMEM((LANES, D_KV), F32),
                        pltpu.VMEM((N_HEADS, 1), F32), pltpu.VMEM((LANES, 1), F32)])
    return pl.pallas_call(
        functools.partial(_fox_sample_kernel, pages_per_step=n), grid_spec=grid_spec,
        out_shape=jax.ShapeDtypeStruct((db, LANES, D_KV), BF16),
        compiler_params=_params("parallel", "arbitrary"), name="fox_sample_attn")(
            page_table, qbd, knew_t, vnew_t, lfnew, *([cache_kt] * n), *([cache_vt] * n), *([cache_lf_t] * n))


def _swa_sample_kernel(qbd_ref, kbuf_ref, vbuf_ref, knew_ref, vnew_ref, sink_ref, o_ref, m_s, l_s, acc_s):
    qbd = qbd_ref[...]
    row_q = lax.broadcasted_iota(jnp.int32, (LANES, LANES), 0) % SUBLANES
    lane = lax.broadcasted_iota(jnp.int32, (LANES, LANES), 1)
    m_s[...] = jnp.full_like(m_s, NEG)
    l_s[...] = jnp.zeros_like(l_s)
    acc_s[...] = jnp.zeros_like(acc_s)
    _decode_update(qbd, kbuf_ref[...], vbuf_ref[...], None, lane > row_q, m_s, l_s, acc_s)
    _decode_update(qbd, knew_ref[...], vnew_ref[...], None, lane <= row_q, m_s, l_s, acc_s)
    l = l_s[...] + jnp.exp(sink_ref[...] - m_s[...])
    o_ref[...] = (acc_s[...] * (1.0 / l)).astype(BF16)


def _swa_sample(qbd, kbuf, vbuf, knew, vnew, sink_rows):
    db = qbd.shape[0]
    seq = lambda b: (b, 0, 0)
    blk = pl.BlockSpec((None, LANES, D_KV), seq)
    return pl.pallas_call(
        _swa_sample_kernel, grid=(db,),
        in_specs=[blk, blk, blk, blk, blk, pl.BlockSpec((LANES, 1), lambda b: (0, 0))],
        out_specs=blk, out_shape=jax.ShapeDtypeStruct((db, LANES, D_KV), BF16),
        scratch_shapes=[pltpu.VMEM((LANES, 1), F32), pltpu.VMEM((LANES, 1), F32), pltpu.VMEM((LANES, D_KV), F32)],
        compiler_params=_params("parallel"), name="swa_sample_attn")(qbd, kbuf, vbuf, knew, vnew, sink_rows)


def _norm_kernel(x_ref, g_ref, h_ref):
    h_ref[...] = _rms(x_ref[...], g_ref[...])


def _norm(x, g, tm=512):
    t = x.shape[0]
    row = lambda i: (i, 0)
    return pl.pallas_call(
        _norm_kernel, grid=(t // tm,),
        in_specs=[pl.BlockSpec((tm, D_MODEL), row), pl.BlockSpec((1, D_MODEL), lambda i: (0, 0))],
        out_specs=pl.BlockSpec((tm, D_MODEL), row), out_shape=jax.ShapeDtypeStruct((t, D_MODEL), F32),
        compiler_params=_params("parallel"), name="pre_norm")(x, g)


def _rwkv_proj_kernel(h_ref, hp_ref, mu_ref, wr_ref, wk_ref, wv_ref, w1_ref, w2_ref, a1_ref, a2_ref, g1_ref, g2_ref,
                      vec_ref, gmat_ref, r_ref, w_ref, k_ref, v_ref, kk_ref, kka_ref, g_ref):
    h = h_ref[...]
    xx = hp_ref[...] - h

    def mix(i):
        return (h + xx * mu_ref[i:i + 1, :]).astype(BF16)

    w0, a0, k_k, k_a = (vec_ref[i:i + 1, :] for i in range(4))
    r = _dot(mix(0), wr_ref[...])
    k = _dot(mix(2), wk_ref[...])
    v = _dot(mix(3), wv_ref[...])
    lw = _dot(jnp.tanh(_dot(mix(1), w1_ref[...])).astype(BF16), w2_ref[...])
    w_log = -_softplus(-(w0 + lw)) - 0.5
    decay = jnp.exp(-jnp.exp(w_log))
    a = jax.nn.sigmoid(a0 + _dot(_dot(mix(4), a1_ref[...]).astype(BF16), a2_ref[...]))
    g = _dot(jax.nn.sigmoid(_dot(mix(5), g1_ref[...])).astype(BF16), g2_ref[...])
    kk = k * k_k
    ss = _split_dot(_tile_sum(kk * kk), gmat_ref[...], 3)
    kk = kk * _tile8(lax.rsqrt(jnp.maximum(ss, 1e-24)))
    r_ref[...] = r
    w_ref[...] = decay
    k_ref[...] = k * (1.0 + (a - 1.0) * k_a)
    v_ref[...] = v
    kk_ref[...] = kk
    kka_ref[...] = kk * a
    g_ref[...] = g


def _rwkv_proj(h, hp, mu, wr, wk, wv, w1, w2, a1, a2, g1, g2, vecs, gmat, tm=256):
    t = h.shape[0]
    row = lambda i: (i, 0)
    fixed = lambda i: (0, 0)
    full = lambda a: pl.BlockSpec(a.shape, fixed)
    tok = pl.BlockSpec((tm, D_MODEL), row)
    consts = [mu, wr, wk, wv, w1, w2, a1, a2, g1, g2, vecs, gmat]
    return pl.pallas_call(
        _rwkv_proj_kernel, grid=(t // tm,), in_specs=[tok, tok] + [full(a) for a in consts],
        out_specs=[tok] * 7, out_shape=[jax.ShapeDtypeStruct((t, D_MODEL), F32)] * 7,
        compiler_params=_params("parallel"), name="rwkv_proj")(h, hp, *consts)


def _rwkv_scan_kernel(*refs, nb, tc):
    ins = refs[:6 * nb]
    r_refs, w_refs, k_refs, v_refs, kk_refs, kka_refs = (ins[i * nb:(i + 1) * nb] for i in range(6))
    s0_ref, gmat_ref, y_ref, sout_ref, s_s = refs[6 * nb:]
    tt = pl.program_id(1)
    groups = HEAD_DIM // SUBLANES
    eye = (lax.broadcasted_iota(jnp.int32, (SUBLANES, LANES), 1) // N_HEADS
           == lax.broadcasted_iota(jnp.int32, (SUBLANES, LANES), 0))
    gmat = gmat_ref[...]

    @pl.when(tt == 0)
    def _():
        s_s[...] = s0_ref[...]

    def bcast(ref, t):
        return jnp.broadcast_to(ref[pl.ds(t, 1), :], (SUBLANES, D_MODEL))

    def step(t, _):
        kk = [bcast(kk_refs[b], t) for b in range(nb)]
        sa_parts = []
        v_parts = []
        for b in range(nb):
            vrow = bcast(v_refs[b], t)
            for g in range(groups):
                sa_parts.append(_tile_sum(s_s[b, g * SUBLANES:(g + 1) * SUBLANES, :] * kk[b]))
                v_parts.append(jnp.where(eye, vrow[:, g * LANES:(g + 1) * LANES], 0.0))
        sa_all = _split_dot(jnp.concatenate(sa_parts, axis=0), gmat, 2)
        v_all = _split_dot(jnp.concatenate(v_parts, axis=0), gmat, 2)
        y_parts = []
        for b in range(nb):
            w = bcast(w_refs[b], t)
            kka = bcast(kka_refs[b], t)
            k = bcast(k_refs[b], t)
            r = bcast(r_refs[b], t)
            for g in range(groups):
                lo = (b * groups + g) * SUBLANES
                sa = _tile8(-sa_all[lo:lo + SUBLANES, :])
                vv = _tile8(v_all[lo:lo + SUBLANES, :])
                s_new = s_s[b, g * SUBLANES:(g + 1) * SUBLANES, :] * w + sa * kka + vv * k
                s_s[b, g * SUBLANES:(g + 1) * SUBLANES, :] = s_new
                y_parts.append(_tile_sum(s_new * r))
        y_all = _split_dot(jnp.concatenate(y_parts, axis=0), gmat, 2)
        t8 = pl.multiple_of((t // SUBLANES) * SUBLANES, SUBLANES)
        mine = lax.broadcasted_iota(jnp.int32, (SUBLANES, LANES), 0) == t % SUBLANES
        for b in range(nb):
            for g in range(groups):
                lo = (b * groups + g) * SUBLANES
                yrow = jnp.sum(jnp.where(eye, y_all[lo:lo + SUBLANES, :], 0.0), axis=0, keepdims=True)
                cur = y_ref[b, pl.ds(t8, SUBLANES), g * LANES:(g + 1) * LANES]
                y_ref[b, pl.ds(t8, SUBLANES), g * LANES:(g + 1) * LANES] = jnp.where(
                    mine, jnp.broadcast_to(yrow, (SUBLANES, LANES)), cur)
        return 0

    y_ref[...] = jnp.zeros_like(y_ref)
    lax.fori_loop(0, tc, step, 0)

    @pl.when(tt == pl.num_programs(1) - 1)
    def _():
        sout_ref[...] = s_s[...]


def _rwkv_scan(vecs, s0, gmat, batch, seq, row0, nb, tc):
    nt = seq // tc
    blk0 = row0 // tc

    def tok(n):
        return pl.BlockSpec((tc, D_MODEL), lambda bb, tt: (blk0 + (bb * nb + n) * nt + tt, 0))

    in_specs = [tok(n) for _ in range(6) for n in range(nb)]
    args = [a for a in vecs for _ in range(nb)]
    state = pl.BlockSpec((nb, HEAD_DIM, D_MODEL), lambda bb, tt: (bb, 0, 0))
    in_specs += [state, pl.BlockSpec((LANES, LANES), lambda bb, tt: (0, 0))]
    return pl.pallas_call(
        functools.partial(_rwkv_scan_kernel, nb=nb, tc=tc), grid=(batch // nb, nt), in_specs=in_specs,
        out_specs=[pl.BlockSpec((nb, tc, D_MODEL), lambda bb, tt: (bb, tt, 0)), state],
        out_shape=[jax.ShapeDtypeStruct((batch, seq, D_MODEL), F32),
                   jax.ShapeDtypeStruct((batch, HEAD_DIM, D_MODEL), F32)],
        scratch_shapes=[pltpu.VMEM((nb, HEAD_DIM, D_MODEL), F32)],
        compiler_params=_params("parallel", "arbitrary"), name="rwkv_scan")(*args, s0, gmat)


def _rwkv_out_kernel(yp_ref, ys_ref, r_ref, k_ref, v_ref, g_ref, x_ref, vec_ref, gmat_ref, wo_ref, gpost_ref, o_ref, *,
                     prompt_tiles):
    r_k, ln_w, ln_b = (vec_ref[i:i + 1, :] for i in range(3))
    gmat = gmat_ref[...]
    y = jnp.where(pl.program_id(0) < prompt_tiles, yp_ref[...], ys_ref[...])
    mean = _split_dot(_tile_sum(y), gmat, 3) * (1.0 / HEAD_DIM)
    yc = y - _tile8(mean)
    var = _split_dot(_tile_sum(yc * yc), gmat, 3) * (1.0 / HEAD_DIM)
    yn = yc * _tile8(lax.rsqrt(var + GN_EPS)) * ln_w + ln_b
    bonus = _split_dot(_tile_sum(r_ref[...] * k_ref[...] * r_k), gmat, 3)
    out = (yn + _tile8(bonus) * v_ref[...]) * g_ref[...]
    m = _dot(out.astype(BF16), wo_ref[...])
    o_ref[...] = x_ref[...] + _rms(m, gpost_ref[...])


def _rwkv_out(y_p, y_s, r, k, v, g, x, vecs, gmat, wo, gpost, tm=256):
    t = x.shape[0]
    n_p = y_p.shape[0] // tm
    row = lambda i: (i, 0)
    fixed = lambda i: (0, 0)
    tok = pl.BlockSpec((tm, D_MODEL), row)
    consts = [vecs, gmat, wo, gpost]
    y_specs = [pl.BlockSpec((tm, D_MODEL), lambda i: (jnp.minimum(i, n_p - 1), 0)),
               pl.BlockSpec((tm, D_MODEL), lambda i: (jnp.maximum(i - n_p, 0), 0))]
    return pl.pallas_call(
        functools.partial(_rwkv_out_kernel, prompt_tiles=n_p), grid=(t // tm,),
        in_specs=y_specs + [tok] * 5 + [pl.BlockSpec(a.shape, fixed) for a in consts],
        out_specs=tok, out_shape=jax.ShapeDtypeStruct((t, D_MODEL), F32),
        compiler_params=_params("parallel"), name="rwkv_out")(y_p, y_s, r, k, v, g, x, *consts)


def _block_diag_q(q_perm, db, ds):
    inv = np.argsort(np.asarray(HEAD_PERM))
    q = q_perm.reshape(db, ds, N_HEADS, HEAD_DIM)[:, :, inv]
    q = q.reshape(db, ds, KV_HEADS, N_HEADS // KV_HEADS, HEAD_DIM).transpose(0, 2, 3, 1, 4)
    eye = jnp.eye(KV_HEADS, dtype=q.dtype)
    qbd = q[:, :, :, :, None, :] * eye[None, :, None, None, :, None]
    return qbd.reshape(db, N_HEADS * ds, D_KV)


def _undo_block_diag(o, db, ds):
    o = o.reshape(db, KV_HEADS, N_HEADS // KV_HEADS, ds, KV_HEADS, HEAD_DIM)
    o = jnp.stack([o[:, kv, :, :, kv, :] for kv in range(KV_HEADS)], axis=1)
    o = o.transpose(0, 3, 1, 2, 4).reshape(db, ds, N_HEADS, HEAD_DIM)
    return o[:, :, np.asarray(HEAD_PERM)].reshape(db * ds, D_MODEL)


def _pad_page(x, db, ds):
    return jnp.pad(x.reshape(db, ds, -1), ((0, 0), (0, PAGE - ds), (0, 0)))


def _rope_tables(pos):
    inv_freq = ROPE_THETA ** (-jnp.arange(0, ROT_DIM, 2, dtype=F32) / ROT_DIM)
    ang = pos.astype(F32)[:, None] * inv_freq[None, :]
    cos, sin = jnp.cos(ang), jnp.sin(ang)
    half = ROT_DIM // 2
    ones = jnp.ones((pos.shape[0], HEAD_DIM - ROT_DIM), F32)
    zeros = jnp.zeros((pos.shape[0], HEAD_DIM - ROT_DIM), F32)
    zh = jnp.zeros_like(sin)
    a = jnp.concatenate([cos, cos, ones], axis=1)
    m = jnp.concatenate([-sin, zh, zeros], axis=1)
    p = jnp.concatenate([zh, sin, zeros], axis=1)
    rep = LANES // HEAD_DIM
    return tuple(jnp.tile(z, (1, rep)) for z in (a, m, p))


def kernel(x_prompt, x_sample, cache_fox_k_l0, cache_fox_v_l0, cache_fox_logf_l0, state_rwkv_wkv_l1, state_rwkv_shift_l1, cache_swa_k_l2, cache_swa_v_l2, cache_fox_k_l3, cache_fox_v_l3, cache_fox_logf_l3, page_table, p_prompt, p_sample, norm_mix_pre, norm_mix_post, norm_ffn_pre, norm_ffn_post, ffn_w_up, ffn_w_down, ple_w_proj, ple_w_gate, fox_w_q, fox_w_k, fox_w_v, fox_w_f, fox_b_f, fox_w_o, rwkv_mu, rwkv_w_r, rwkv_w_k, rwkv_w_v, rwkv_w_o, rwkv_w0, rwkv_w1, rwkv_w2, rwkv_a0, rwkv_a1, rwkv_a2, rwkv_g1, rwkv_g2, rwkv_k_k, rwkv_k_a, rwkv_r_k, rwkv_ln_w, rwkv_ln_b, swa_w_q, swa_b_q, swa_w_k, swa_b_k, swa_w_v, swa_b_v, swa_sinks, swa_w_o):
    bp, sp, _ = x_prompt.shape
    db, ds, _ = x_sample.shape
    depth = norm_mix_pre.shape[0]
    tp, ts = bp * sp, db * ds
    n_pages = page_table.shape[1]
    past = n_pages * PAGE
    bf = lambda a: a.astype(BF16)
    vec = lambda a: a.reshape(1, -1).astype(F32)

    x = jnp.concatenate([x_prompt.reshape(tp, D_MODEL), x_sample.reshape(ts, D_MODEL)], axis=0)
    pp_all = p_prompt.reshape(depth * tp, D_PLE)
    ps_all = p_sample.reshape(depth * ts, D_PLE)
    pos = jnp.concatenate([jnp.tile(jnp.arange(sp), bp), jnp.tile(past + jnp.arange(ds), db)])
    gmat = (np.arange(LANES)[:, None] % N_HEADS == np.arange(LANES)[None, :] % N_HEADS)
    gmat = jnp.asarray(gmat, BF16)
    fox_caches = ((cache_fox_k_l0, cache_fox_v_l0, cache_fox_logf_l0), (cache_fox_k_l3, cache_fox_v_l3, cache_fox_logf_l3))
    new_state = []

    for i in range(depth):
        kind, j = i % N_MIXERS, i // N_MIXERS
        g_pre = vec(norm_mix_pre[i])
        g_post = vec(norm_mix_post[i])
        if kind == 0:
            wf = jnp.pad(fox_w_f[j], ((0, 0), (0, LANES - N_HEADS)))
            w = bf(jnp.concatenate([fox_w_q[j][:, Q_COLS], fox_w_k[j], fox_w_v[j], wf], axis=1))
            b = jnp.concatenate([jnp.zeros((D_MODEL + 2 * D_KV,), F32), fox_b_f[j], jnp.zeros((LANES - N_HEADS,), F32)])
            q, k, v, lf = _proj(x, g_pre, w, vec(b), None)
            lf = lf[:, :N_HEADS]
            lf_p = lf[:tp].reshape(bp, sp, N_HEADS)
            ck = _cumsum(lf_p.transpose(0, 2, 1))
            cq = ck.transpose(0, 2, 1).reshape(tp, N_HEADS)
            o_p = _flash(q, k, v, cq, ck, bp, sp)
            ck_cache, cv_cache, clf_cache = fox_caches[j]
            n_pool = ck_cache.shape[0]
            pages_t = lambda c: c.transpose(0, 2, 3, 1).reshape(n_pool, D_KV, PAGE)
            new_t = lambda z: _pad_page(z, db, ds).transpose(0, 2, 1)
            o_s = _fox_sample(
                page_table, _block_diag_q(q[tp:], db, ds), new_t(k[tp:]), new_t(v[tp:]), new_t(lf[tp:]),
                pages_t(ck_cache), pages_t(cv_cache), clf_cache.transpose(0, 2, 1))
            x = _oproj(o_p, _undo_block_diag(o_s, db, ds), bf(fox_w_o[j][Q_COLS, :]), g_post, x)
            new_state.append((k[:tp].reshape(bp, sp, KV_HEADS, HEAD_DIM), v[:tp].reshape(bp, sp, KV_HEADS, HEAD_DIM),
                              lf_p, k[tp:].reshape(db, ds, KV_HEADS, HEAD_DIM),
                              v[tp:].reshape(db, ds, KV_HEADS, HEAD_DIM), lf[tp:].reshape(db, ds, N_HEADS)))
        elif kind == 1:
            h = _norm(x, g_pre)
            h_p = h[:tp].reshape(bp, sp, D_MODEL)
            h_s = h[tp:].reshape(db, ds, D_MODEL)
            hp = jnp.concatenate([
                jnp.concatenate([jnp.zeros((bp, 1, D_MODEL), F32), h_p[:, :-1]], axis=1).reshape(tp, D_MODEL),
                jnp.concatenate([state_rwkv_shift_l1[:, None, :], h_s[:, :-1]], axis=1).reshape(ts, D_MODEL)], axis=0)
            pc = lambda a: a[:, R_COLS]
            vecs = jnp.stack([rwkv_w0[j][R_COLS], rwkv_a0[j][R_COLS], rwkv_k_k[j][R_COLS], rwkv_k_a[j][R_COLS]])
            r, w, k, v, kk, kka, g = _rwkv_proj(
                h, hp, rwkv_mu[j], bf(pc(rwkv_w_r[j])), bf(pc(rwkv_w_k[j])), bf(pc(rwkv_w_v[j])), bf(rwkv_w1[j]),
                bf(pc(rwkv_w2[j])), bf(rwkv_a1[j]), bf(pc(rwkv_a2[j])), bf(rwkv_g1[j]), bf(pc(rwkv_g2[j])), vecs, gmat)
            scan_in = (r, w, k, v, kk, kka)
            to_lanes = lambda s: s.transpose(0, 2, 3, 1).reshape(s.shape[0], HEAD_DIM, D_MODEL)
            from_lanes = lambda s: s.reshape(s.shape[0], HEAD_DIM, HEAD_DIM, N_HEADS).transpose(0, 3, 1, 2)
            y_p, s_p = _rwkv_scan(scan_in, jnp.zeros((bp, HEAD_DIM, D_MODEL), F32), gmat, bp, sp, 0, bp, 64)
            y_s, s_s = _rwkv_scan(scan_in, to_lanes(state_rwkv_wkv_l1.astype(F32)), gmat, db, ds, tp, 4, ds)
            ovecs = jnp.stack([rwkv_r_k[j].reshape(-1)[R_COLS], rwkv_ln_w[j][R_COLS], rwkv_ln_b[j][R_COLS]])
            x = _rwkv_out(y_p.reshape(tp, D_MODEL), y_s.reshape(ts, D_MODEL), r, k, v, g, x, ovecs, gmat,
                          bf(rwkv_w_o[j][R_COLS, :]), g_post)
            new_state.append((from_lanes(s_p), h_p[:, -1], from_lanes(s_s), h_s[:, -1]))
        else:
            w = bf(jnp.concatenate([swa_w_q[j][:, Q_COLS], swa_w_k[j], swa_w_v[j]], axis=1))
            b = jnp.concatenate([swa_b_q[j][Q_COLS], swa_b_k[j], swa_b_v[j]])
            q, k, v = _proj(x, g_pre, w, vec(b), _rope_tables(pos))
            o_p = _swa_band(q, k, v, vec(swa_sinks[j]), bp, sp)
            keep = cache_swa_k_l2.shape[1]
            k_s = k[tp:].reshape(db, ds, D_KV)
            v_s = v[tp:].reshape(db, ds, D_KV)
            kbuf = cache_swa_k_l2.reshape(db, keep, D_KV)
            vbuf = cache_swa_v_l2.reshape(db, keep, D_KV)
            sink_rows = jnp.repeat(swa_sinks[j].astype(F32), ds).reshape(N_HEADS * ds, 1)
            o_s = _swa_sample(_block_diag_q(q[tp:], db, ds), kbuf, vbuf, _pad_page(k[tp:], db, ds),
                              _pad_page(v[tp:], db, ds), sink_rows)
            x = _oproj(o_p, _undo_block_diag(o_s, db, ds), bf(swa_w_o[j][Q_COLS, :]), g_post, x)
            k_p = k[:tp].reshape(bp, sp, KV_HEADS, HEAD_DIM)
            v_p = v[:tp].reshape(bp, sp, KV_HEADS, HEAD_DIM)
            wk = min(WINDOW, sp)
            new_state.append((k_p[:, sp - wk:], v_p[:, sp - wk:],
                              jnp.concatenate([kbuf, k_s], axis=1)[:, ds:].reshape(db, keep, KV_HEADS, HEAD_DIM),
                              jnp.concatenate([vbuf, v_s], axis=1)[:, ds:].reshape(db, keep, KV_HEADS, HEAD_DIM)))
        x = _ffn(x, vec(norm_ffn_pre[i]), bf(ffn_w_up[i]), bf(ffn_w_down[i]), vec(norm_ffn_post[i]), pp_all, ps_all, i,
                 tp, bf(ple_w_proj[i]), bf(ple_w_gate[i]))

    outs = [x[:tp].reshape(bp, sp, D_MODEL), x[tp:].reshape(db, ds, D_MODEL)]
    for st in new_state:
        outs.extend(st)
    return tuple(outs)
```

```python
import functools

import numpy as np
import jax
import jax.numpy as jnp
from jax import lax
from jax.experimental import pallas as pl
from jax.experimental.pallas import tpu as pltpu

F32 = jnp.float32
BF16 = jnp.bfloat16

D_MODEL = 1024
HEAD_DIM = 64
N_HEADS = 16
KV_HEADS = 4
D_KV = KV_HEADS * HEAD_DIM
D_FF = 4096
D_PLE = 256
PAGE = 128
WINDOW = 128
ROT_DIM = 16
ROPE_THETA = 500000.0
NORM_EPS = 1e-6
GN_EPS = 64e-5
N_MIXERS = 3
LANES = 128
SUBLANES = 8
NEG = -1e30
VMEM_LIMIT = 48 * 1024 * 1024
SCALE = HEAD_DIM ** -0.5
C_PIECES = 3

HEAD_PERM = (0, 4, 1, 5, 2, 6, 3, 7, 8, 12, 9, 13, 10, 14, 11, 15)
Q_COLS = np.concatenate([np.arange(HEAD_DIM) + HEAD_DIM * h for h in HEAD_PERM])
R_COLS = (np.arange(D_MODEL) % N_HEADS) * HEAD_DIM + np.arange(D_MODEL) // N_HEADS


def _params(*sem):
    return pltpu.CompilerParams(dimension_semantics=sem, vmem_limit_bytes=VMEM_LIMIT)


def _dot(a, b):
    return jnp.dot(a, b, preferred_element_type=F32)


def _dot_nt(a, b):
    return lax.dot_general(a, b, (((1,), (1,)), ((), ())), preferred_element_type=F32)


def _rms(x, g):
    return x * lax.rsqrt(jnp.mean(x * x, axis=-1, keepdims=True) + NORM_EPS) * g


def _log_sigmoid(x):
    return jnp.minimum(x, 0.0) - jnp.log1p(jnp.exp(-jnp.abs(x)))


def _softplus(x):
    return jnp.maximum(x, 0.0) + jnp.log1p(jnp.exp(-jnp.abs(x)))


def _split_dot(x, g, parts):
    out = None
    for _ in range(parts):
        piece = x.astype(BF16)
        term = _dot(piece, g)
        out = term if out is None else out + term
        x = x - piece.astype(F32)
    return out


def _tile_sum(x):
    out = x[:, 0:LANES]
    for t in range(1, D_MODEL // LANES):
        out = out + x[:, t * LANES:(t + 1) * LANES]
    return out


def _tile8(x):
    return jnp.concatenate([x] * (D_MODEL // LANES), axis=1)


def _rope(x, a, m, p):
    chunks = []
    for c in range(x.shape[1] // LANES):
        xc = x[:, c * LANES:(c + 1) * LANES]
        chunks.append(xc * a + pltpu.roll(xc, LANES - ROT_DIM // 2, 1) * m + pltpu.roll(xc, ROT_DIM // 2, 1) * p)
    return jnp.concatenate(chunks, axis=1)


def _proj_kernel(x_ref, g_ref, w_ref, b_ref, *rest, rope):
    if rope:
        ra_ref, rm_ref, rp_ref, q_ref, k_ref, v_ref = rest
    else:
        q_ref, k_ref, v_ref, lf_ref = rest
    h = _rms(x_ref[...], g_ref[...]).astype(BF16)
    res = _dot(h, w_ref[...]) + b_ref[...]
    q = res[:, :D_MODEL]
    k = res[:, D_MODEL:D_MODEL + D_KV]
    v = res[:, D_MODEL + D_KV:D_MODEL + 2 * D_KV]
    if rope:
        a, m, p = ra_ref[...], rm_ref[...], rp_ref[...]
        q = _rope(q, a, m, p)
        k = _rope(k, a, m, p)
    else:
        lf_ref[...] = _log_sigmoid(res[:, D_MODEL + 2 * D_KV:])
    q_ref[...] = (q * SCALE).astype(BF16)
    k_ref[...] = k
    v_ref[...] = v


def _proj(x, g, w, b, rope_tabs, tm=512):
    t = x.shape[0]
    n = w.shape[1]
    rope = rope_tabs is not None
    row = lambda i: (i, 0)
    fixed = lambda i: (0, 0)
    in_specs = [pl.BlockSpec((tm, D_MODEL), row), pl.BlockSpec((1, D_MODEL), fixed),
                pl.BlockSpec((D_MODEL, n), fixed), pl.BlockSpec((1, n), fixed)]
    args = [x, g, w, b]
    out_shape = [jax.ShapeDtypeStruct((t, D_MODEL), BF16), jax.ShapeDtypeStruct((t, D_KV), F32),
                 jax.ShapeDtypeStruct((t, D_KV), F32)]
    out_specs = [pl.BlockSpec((tm, D_MODEL), row), pl.BlockSpec((tm, D_KV), row), pl.BlockSpec((tm, D_KV), row)]
    if rope:
        in_specs += [pl.BlockSpec((tm, LANES), row)] * 3
        args += list(rope_tabs)
    else:
        out_shape.append(jax.ShapeDtypeStruct((t, LANES), F32))
        out_specs.append(pl.BlockSpec((tm, LANES), row))
    return pl.pallas_call(
        functools.partial(_proj_kernel, rope=rope), grid=(t // tm,), in_specs=in_specs, out_specs=out_specs,
        out_shape=out_shape, compiler_params=_params("parallel"),
        name="swa_proj" if rope else "fox_proj")(*args)


def _oproj_kernel(op_ref, os_ref, w_ref, g_ref, x_ref, y_ref, *, prompt_tiles):
    o = jnp.where(pl.program_id(0) < prompt_tiles, op_ref[...], os_ref[...])
    m = _dot(o, w_ref[...])
    y_ref[...] = x_ref[...] + _rms(m, g_ref[...])


def _oproj(o_p, o_s, w, g, x, tm=512):
    t = x.shape[0]
    n_p = o_p.shape[0] // tm
    row = lambda i: (i, 0)
    fixed = lambda i: (0, 0)
    return pl.pallas_call(
        functools.partial(_oproj_kernel, prompt_tiles=n_p), grid=(t // tm,),
        in_specs=[pl.BlockSpec((tm, D_MODEL), lambda i: (jnp.minimum(i, n_p - 1), 0)),
                  pl.BlockSpec((tm, D_MODEL), lambda i: (jnp.maximum(i - n_p, 0), 0)),
                  pl.BlockSpec((D_MODEL, D_MODEL), fixed), pl.BlockSpec((1, D_MODEL), fixed),
                  pl.BlockSpec((tm, D_MODEL), row)],
        out_specs=pl.BlockSpec((tm, D_MODEL), row), out_shape=jax.ShapeDtypeStruct((t, D_MODEL), F32),
        compiler_params=_params("parallel"), name="out_proj")(o_p, o_s, w, g, x)


def _ffn_kernel(x_ref, gpre_ref, wup_ref, wdn_ref, gpost_ref, pp_ref, ps_ref, wp_ref, wg_ref, y_ref, *, prompt_tiles, tf):
    x = x_ref[...]
    h = _rms(x, gpre_ref[...]).astype(BF16)
    acc = None
    for c in range(D_FF // tf):
        u = jnp.maximum(_dot(h, wup_ref[:, c * tf:(c + 1) * tf]), 0.0)
        part = _dot((u * u).astype(BF16), wdn_ref[c * tf:(c + 1) * tf, :])
        acc = part if acc is None else acc + part
    x1 = x + _rms(acc, gpost_ref[...])
    gate = jax.nn.sigmoid(_dot(x1.astype(BF16), wg_ref[...]))
    p = jnp.where(pl.program_id(0) < prompt_tiles, pp_ref[...], ps_ref[...])
    y_ref[...] = x1 + _dot(p.astype(BF16), wp_ref[...]) * gate


def _ffn(x, gpre, wup, wdn, gpost, p_prompt, p_sample, layer, tp, wp, wg, tm=512, tf=512):
    t = x.shape[0]
    n_p = tp // tm
    n_s = (t - tp) // tm
    row = lambda i: (i, 0)
    fixed = lambda i: (0, 0)
    once = dict(pipeline_mode=pl.Buffered(1))
    return pl.pallas_call(
        functools.partial(_ffn_kernel, prompt_tiles=n_p, tf=tf), grid=(t // tm,),
        in_specs=[pl.BlockSpec((tm, D_MODEL), row), pl.BlockSpec((1, D_MODEL), fixed),
                  pl.BlockSpec((D_MODEL, D_FF), fixed, **once), pl.BlockSpec((D_FF, D_MODEL), fixed, **once),
                  pl.BlockSpec((1, D_MODEL), fixed),
                  pl.BlockSpec((tm, D_PLE), lambda i: (layer * n_p + jnp.minimum(i, n_p - 1), 0)),
                  pl.BlockSpec((tm, D_PLE), lambda i: (layer * n_s + jnp.maximum(i - n_p, 0), 0)),
                  pl.BlockSpec((D_PLE, D_MODEL), fixed, **once), pl.BlockSpec((D_MODEL, D_MODEL), fixed, **once)],
        out_specs=pl.BlockSpec((tm, D_MODEL), row), out_shape=jax.ShapeDtypeStruct((t, D_MODEL), F32),
        compiler_params=_params("parallel"), name="ffn_ple")(x, gpre, wup, wdn, gpost, p_prompt, p_sample, wp, wg)


def _cumsum_kernel(x_ref, c_ref):
    n = x_ref.shape[1]
    lane = lax.broadcasted_iota(jnp.int32, (N_HEADS, LANES), 1)
    carry = jnp.zeros((N_HEADS, 1), F32)
    for c in range(n // LANES):
        blk = x_ref[:, c * LANES:(c + 1) * LANES]
        sh = 1
        while sh < LANES:
            blk = blk + jnp.where(lane >= sh, pltpu.roll(blk, sh, 1), 0.0)
            sh *= 2
        blk = blk + carry
        c_ref[:, c * LANES:(c + 1) * LANES] = blk
        carry = blk[:, LANES - 1:LANES]


def _cumsum(x):
    b, h, t = x.shape
    spec = pl.BlockSpec((None, h, t), lambda i: (i, 0, 0))
    return pl.pallas_call(_cumsum_kernel, grid=(b,), in_specs=[spec], out_specs=spec,
                          out_shape=jax.ShapeDtypeStruct(x.shape, F32), compiler_params=_params("parallel"),
                          name="logf_cumsum")(x)


def _flash_kernel(qi_tab, ki_tab, first_tab, last_tab, q_ref, k_ref, v_ref, cq_ref, crep_ref, o_ref, qm_s, m_s, l_s, acc_s,
                  *, tq, tk):
    step = pl.program_id(1)
    qi = qi_tab[step]
    ki = ki_tab[step]
    lane = lax.broadcasted_iota(jnp.int32, (tq, LANES), 1)
    rep = tk // LANES

    @pl.when(first_tab[step] == 1)
    def _():
        m_s[...] = jnp.full_like(m_s, NEG)
        l_s[...] = jnp.zeros_like(l_s)
        acc_s[...] = jnp.zeros_like(acc_s)
        for pp in range(N_HEADS):
            h = HEAD_PERM[pp]
            qp = q_ref[:, (pp // 2) * LANES:(pp // 2 + 1) * LANES]
            keep = (lane < HEAD_DIM) if pp % 2 == 0 else (lane >= HEAD_DIM)
            qm_s[pp * tq:(pp + 1) * tq, 0:LANES] = jnp.where(keep, qp, jnp.zeros_like(qp))
            ones = (lane % N_HEADS == h) & (lane < C_PIECES * N_HEADS)
            qm_s[pp * tq:(pp + 1) * tq, LANES:2 * LANES] = jnp.where(ones, 1.0, 0.0).astype(BF16)

    kb = k_ref[...].astype(BF16)
    vb = v_ref[...].astype(BF16)
    klane = lax.broadcasted_iota(jnp.int32, (tk, LANES), 1)
    rest = -crep_ref[...]
    cneg = jnp.zeros((tk, LANES), BF16)
    for p in range(C_PIECES):
        piece = rest.astype(BF16)
        cneg = jnp.where((klane >= p * N_HEADS) & (klane < (p + 1) * N_HEADS), piece, cneg)
        rest = rest - piece.astype(F32)
    row = qi * tq + lax.broadcasted_iota(jnp.int32, (tq, tk), 0)
    col = ki * tk + lax.broadcasted_iota(jnp.int32, (tq, tk), 1)
    group = N_HEADS // 2
    mask = jnp.concatenate([col <= row] * group, axis=0)
    for g0 in range(0, N_HEADS, group):
        kv = g0 // group
        kp = jnp.concatenate([kb[:, kv * LANES:(kv + 1) * LANES], cneg], axis=1)
        vp = vb[:, kv * LANES:(kv + 1) * LANES]
        rows = slice(g0 * tq, (g0 + group) * tq)
        s = jnp.where(mask, _dot_nt(qm_s[rows, :], kp), NEG)
        cq = jnp.concatenate([jnp.broadcast_to(cq_ref[:, HEAD_PERM[g0 + i]:HEAD_PERM[g0 + i] + 1], (tq, LANES))
                              for i in range(group)], axis=0)
        m_old = m_s[rows, :]
        m_new = jnp.maximum(m_old, jnp.max(s, axis=1, keepdims=True) + cq)
        alpha = jnp.exp(m_old - m_new)
        p = jnp.exp(s - jnp.concatenate([m_new - cq] * rep, axis=1))
        l_s[rows, :] = alpha * l_s[rows, :] + jnp.sum(p, axis=1, keepdims=True)
        m_s[rows, :] = m_new
        pv = _dot(p.astype(BF16), vp)
        for i in range(0, group, 2):
            pair = (g0 + i) // 2
            a = acc_s[:, pair * LANES:(pair + 1) * LANES]
            a0, a1 = alpha[i * tq:(i + 1) * tq, :], alpha[(i + 1) * tq:(i + 2) * tq, :]
            acc_s[:, pair * LANES:(pair + 1) * LANES] = jnp.where(
                lane < HEAD_DIM, a0 * a + pv[i * tq:(i + 1) * tq, :], a1 * a + pv[(i + 1) * tq:(i + 2) * tq, :])

    @pl.when(last_tab[step] == 1)
    def _():
        for pair in range(N_HEADS // 2):
            l0 = l_s[(2 * pair) * tq:(2 * pair + 1) * tq, :]
            l1 = l_s[(2 * pair + 1) * tq:(2 * pair + 2) * tq, :]
            inv = jnp.where(lane < HEAD_DIM, 1.0 / l0, 1.0 / l1)
            o_ref[:, pair * LANES:(pair + 1) * LANES] = (acc_s[:, pair * LANES:(pair + 1) * LANES] * inv).astype(BF16)


def _flash(q, k, v, cq, batch, seq, tq=512):
    crep = jnp.concatenate([cq] * C_PIECES + [jnp.zeros((cq.shape[0], LANES - C_PIECES * N_HEADS), F32)], axis=1)
    tk = tq
    nq = seq // tq
    pairs = [(a, b) for a in range(nq) for b in range(a + 1)]
    qi_tab = jnp.asarray([a for a, _ in pairs], jnp.int32)
    ki_tab = jnp.asarray([b for _, b in pairs], jnp.int32)
    first_tab = jnp.asarray([int(b == 0) for a, b in pairs], jnp.int32)
    last_tab = jnp.asarray([int(b == a) for a, b in pairs], jnp.int32)
    qmap = lambda b, s, qt, kt, ft, lt: (b * nq + qt[s], 0)
    kmap = lambda b, s, qt, kt, ft, lt: (b * nq + kt[s], 0)
    in_specs = [pl.BlockSpec((tq, D_MODEL), qmap), pl.BlockSpec((tk, D_KV), kmap), pl.BlockSpec((tk, D_KV), kmap),
                pl.BlockSpec((tq, N_HEADS), qmap), pl.BlockSpec((tk, LANES), kmap)]
    grid_spec = pltpu.PrefetchScalarGridSpec(
        num_scalar_prefetch=4, grid=(batch, len(pairs)), in_specs=in_specs,
        out_specs=pl.BlockSpec((tq, D_MODEL), qmap),
        scratch_shapes=[pltpu.VMEM((N_HEADS * tq, 2 * LANES), BF16), pltpu.VMEM((N_HEADS * tq, LANES), F32),
                        pltpu.VMEM((N_HEADS * tq, LANES), F32), pltpu.VMEM((tq, D_MODEL), F32)])
    return pl.pallas_call(
        functools.partial(_flash_kernel, tq=tq, tk=tk), grid_spec=grid_spec,
        out_shape=jax.ShapeDtypeStruct((batch * seq, D_MODEL), BF16),
        compiler_params=_params("parallel", "arbitrary"),
        name="fox_prompt_attn")(qi_tab, ki_tab, first_tab, last_tab, q, k, v, cq, crep)


def _swa_band_kernel(q_ref, kp_ref, ko_ref, vp_ref, vo_ref, sink_ref, o_ref, *, tq):
    qi = pl.program_id(1)
    nk = WINDOW + tq
    kb = jnp.concatenate([kp_ref[...], ko_ref[...]], axis=0).astype(BF16)
    vb = jnp.concatenate([vp_ref[...], vo_ref[...]], axis=0).astype(BF16)
    lane = lax.broadcasted_iota(jnp.int32, (tq, LANES), 1)
    a = lax.broadcasted_iota(jnp.int32, (tq, nk), 0)
    c = lax.broadcasted_iota(jnp.int32, (tq, nk), 1)
    mask = (c > a) & (c <= a + WINDOW) & (qi * tq + c >= WINDOW)
    for pair in range(N_HEADS // 2):
        kv = pair // 4
        kp = kb[:, kv * LANES:(kv + 1) * LANES]
        vp = vb[:, kv * LANES:(kv + 1) * LANES]
        qp = q_ref[:, pair * LANES:(pair + 1) * LANES]
        outs = []
        for half in range(2):
            h = HEAD_PERM[2 * pair + half]
            keep = (lane < HEAD_DIM) if half == 0 else (lane >= HEAD_DIM)
            s = _dot_nt(jnp.where(keep, qp, jnp.zeros_like(qp)), kp)
            s = jnp.where(mask, s, NEG)
            sink = sink_ref[:, h:h + 1]
            m = jnp.maximum(jnp.max(s, axis=1, keepdims=True), sink)
            p = jnp.exp(s - m)
            l = jnp.sum(p, axis=1, keepdims=True) + jnp.exp(sink - m)
            outs.append(_dot(p.astype(BF16), vp) * (1.0 / l))
        o_ref[:, pair * LANES:(pair + 1) * LANES] = jnp.where(lane < HEAD_DIM, outs[0], outs[1]).astype(BF16)


def _swa_band(q, k, v, sinks, batch, seq, tq=256):
    nq = seq // tq
    r = tq // WINDOW
    qmap = lambda b, i: (b * nq + i, 0)
    pmap = lambda b, i: (jnp.maximum((b * nq + i) * r - 1, 0), 0)
    return pl.pallas_call(
        functools.partial(_swa_band_kernel, tq=tq), grid=(batch, nq),
        in_specs=[pl.BlockSpec((tq, D_MODEL), qmap), pl.BlockSpec((WINDOW, D_KV), pmap), pl.BlockSpec((tq, D_KV), qmap),
                  pl.BlockSpec((WINDOW, D_KV), pmap), pl.BlockSpec((tq, D_KV), qmap),
                  pl.BlockSpec((1, N_HEADS), lambda b, i: (0, 0))],
        out_specs=pl.BlockSpec((tq, D_MODEL), qmap), out_shape=jax.ShapeDtypeStruct((batch * seq, D_MODEL), BF16),
        compiler_params=_params("parallel", "parallel"), name="swa_prompt_attn")(q, k, k, v, v, sinks)


def _expand_heads(x16):
    return jnp.concatenate([jnp.broadcast_to(x16[h:h + 1, :], (SUBLANES, LANES)) for h in range(N_HEADS)], axis=0)


def _decode_update(qbd, k, v, bias, mask, m_s, l_s, acc_s):
    s = _dot_nt(qbd, k.astype(BF16))
    if bias is not None:
        s = s + bias
    if mask is not None:
        s = jnp.where(mask, s, NEG)
    m_old = m_s[...]
    m_new = jnp.maximum(m_old, jnp.max(s, axis=1, keepdims=True))
    alpha = jnp.exp(m_old - m_new)
    p = jnp.exp(s - m_new)
    l_s[...] = alpha * l_s[...] + jnp.sum(p, axis=1, keepdims=True)
    m_s[...] = m_new
    acc_s[...] = alpha * acc_s[...] + _dot(p.astype(BF16), v.astype(BF16))


def _suffix_sums(lf):
    lane = lax.broadcasted_iota(jnp.int32, lf.shape, 1)
    x = lf
    sh = 1
    while sh < LANES:
        x = x + jnp.where(lane < LANES - sh, pltpu.roll(x, LANES - sh, 1), 0.0)
        sh *= 2
    return x


def _fox_sample_kernel(pt_ref, q_ref, knew_ref, vnew_ref, lfnew_ref, *rest, pages_per_step):
    n = pages_per_step
    k_refs, v_refs, lf_refs = rest[:n], rest[n:2 * n], rest[2 * n:3 * n]
    o_ref, m_s, l_s, acc_s, carry_s, rowc_s = rest[3 * n:]
    j = pl.program_id(1)
    row_q = lax.broadcasted_iota(jnp.int32, (LANES, LANES), 0) % SUBLANES
    lane = lax.broadcasted_iota(jnp.int32, (LANES, LANES), 1)

    def update(kt, vt, bias, mask):
        s = _dot(q_ref[...], kt) + bias
        if mask is not None:
            s = jnp.where(mask, s, NEG)
        m_old = m_s[...]
        m_new = jnp.maximum(m_old, jnp.max(s, axis=1, keepdims=True))
        alpha = jnp.exp(m_old - m_new)
        p = jnp.exp(s - m_new)
        l_s[...] = alpha * l_s[...] + jnp.sum(p, axis=1, keepdims=True)
        m_s[...] = m_new
        acc_s[...] = alpha * acc_s[...] + _dot_nt(p.astype(BF16), vt)

    @pl.when(j == 0)
    def _():
        m_s[...] = jnp.full_like(m_s, NEG)
        l_s[...] = jnp.zeros_like(l_s)
        acc_s[...] = jnp.zeros_like(acc_s)
        lf = lfnew_ref[...]
        inc = _suffix_sums(lf)
        bias = _expand_heads(inc - lf)
        rowc = -jnp.sum(jnp.where(lane == row_q, bias, 0.0), axis=1, keepdims=True)
        rowc_s[...] = rowc
        carry_s[...] = inc[:, 0:1]
        update(knew_ref[...].astype(BF16), vnew_ref[...].astype(BF16), bias + rowc, lane <= row_q)

    carry = carry_s[...]
    biases = []
    for i in range(n):
        lf = lf_refs[i][...]
        inc = _suffix_sums(lf)
        biases.append(_expand_heads(inc - lf + carry))
        carry = carry + inc[:, 0:1]
    carry_s[...] = carry
    kt = jnp.concatenate([r[...].astype(BF16) for r in k_refs], axis=1)
    vt = jnp.concatenate([r[...].astype(BF16) for r in v_refs], axis=1)
    update(kt, vt, jnp.concatenate(biases, axis=1) + rowc_s[...], None)

    @pl.when(j == pl.num_programs(1) - 1)
    def _():
        o_ref[...] = (acc_s[...] * (1.0 / l_s[...])).astype(BF16)


def _fox_sample(page_table, qbd, knew_t, vnew_t, lfnew, cache_kt, cache_vt, cache_lf_t, pages_per_step=32):
    db, n_pages = page_table.shape
    n = pages_per_step
    seq = lambda b, j, pt: (b, 0, 0)

    def page(i):
        return lambda b, j, pt: (pt[b, n_pages - 1 - (n * j + i)], 0, 0)

    in_specs = [pl.BlockSpec((None, LANES, D_KV), seq), pl.BlockSpec((None, D_KV, PAGE), seq),
                pl.BlockSpec((None, D_KV, PAGE), seq), pl.BlockSpec((None, N_HEADS, PAGE), seq)]
    in_specs += [pl.BlockSpec((None, D_KV, PAGE), page(i)) for i in range(n)]
    in_specs += [pl.BlockSpec((None, D_KV, PAGE), page(i)) for i in range(n)]
    in_specs += [pl.BlockSpec((None, N_HEADS, PAGE), page(i)) for i in range(n)]
    grid_spec = pltpu.PrefetchScalarGridSpec(
        num_scalar_prefetch=1, grid=(db, n_pages // n), in_specs=in_specs,
        out_specs=pl.BlockSpec((None, LANES, D_KV), seq),
        scratch_shapes=[pltpu.VMEM((LANES, 1), F32), pltpu.VMEM((LANES, 1), F32), pltpu.VMEM((LANES, D_KV), F32),
                        pltpu.VMEM((N_HEADS, 1), F32), pltpu.VMEM((LANES, 1), F32)])
    return pl.pallas_call(
        functools.partial(_fox_sample_kernel, pages_per_step=n), grid_spec=grid_spec,
        out_shape=jax.ShapeDtypeStruct((db, LANES, D_KV), BF16),
        compiler_params=_params("parallel", "arbitrary"), name="fox_sample_attn")(
            page_table, qbd, knew_t, vnew_t, lfnew, *([cache_kt] * n), *([cache_vt] * n), *([cache_lf_t] * n))


def _swa_sample_kernel(qbd_ref, kbuf_ref, vbuf_ref, knew_ref, vnew_ref, sink_ref, o_ref, m_s, l_s, acc_s):
    qbd = qbd_ref[...]
    row_q = lax.broadcasted_iota(jnp.int32, (LANES, LANES), 0) % SUBLANES
    lane = lax.broadcasted_iota(jnp.int32, (LANES, LANES), 1)
    m_s[...] = jnp.full_like(m_s, NEG)
    l_s[...] = jnp.zeros_like(l_s)
    acc_s[...] = jnp.zeros_like(acc_s)
    _decode_update(qbd, kbuf_ref[...], vbuf_ref[...], None, lane > row_q, m_s, l_s, acc_s)
    _decode_update(qbd, knew_ref[...], vnew_ref[...], None, lane <= row_q, m_s, l_s, acc_s)
    l = l_s[...] + jnp.exp(sink_ref[...] - m_s[...])
    o_ref[...] = (acc_s[...] * (1.0 / l)).astype(BF16)


def _swa_sample(qbd, kbuf, vbuf, knew, vnew, sink_rows):
    db = qbd.shape[0]
    seq = lambda b: (b, 0, 0)
    blk = pl.BlockSpec((None, LANES, D_KV), seq)
    return pl.pallas_call(
        _swa_sample_kernel, grid=(db,),
        in_specs=[blk, blk, blk, blk, blk, pl.BlockSpec((LANES, 1), lambda b: (0, 0))],
        out_specs=blk, out_shape=jax.ShapeDtypeStruct((db, LANES, D_KV), BF16),
        scratch_shapes=[pltpu.VMEM((LANES, 1), F32), pltpu.VMEM((LANES, 1), F32), pltpu.VMEM((LANES, D_KV), F32)],
        compiler_params=_params("parallel"), name="swa_sample_attn")(qbd, kbuf, vbuf, knew, vnew, sink_rows)


def _norm_kernel(x_ref, g_ref, h_ref):
    h_ref[...] = _rms(x_ref[...], g_ref[...])


def _norm(x, g, tm=512):
    t = x.shape[0]
    row = lambda i: (i, 0)
    return pl.pallas_call(
        _norm_kernel, grid=(t // tm,),
        in_specs=[pl.BlockSpec((tm, D_MODEL), row), pl.BlockSpec((1, D_MODEL), lambda i: (0, 0))],
        out_specs=pl.BlockSpec((tm, D_MODEL), row), out_shape=jax.ShapeDtypeStruct((t, D_MODEL), F32),
        compiler_params=_params("parallel"), name="pre_norm")(x, g)


def _rwkv_proj_kernel(h_ref, hp_ref, mu_ref, wr_ref, wk_ref, wv_ref, w1_ref, w2_ref, a1_ref, a2_ref, g1_ref, g2_ref,
                      vec_ref, gmat_ref, r_ref, w_ref, k_ref, v_ref, kk_ref, kka_ref, g_ref):
    h = h_ref[...]
    xx = hp_ref[...] - h

    def mix(i):
        return (h + xx * mu_ref[i:i + 1, :]).astype(BF16)

    w0, a0, k_k, k_a = (vec_ref[i:i + 1, :] for i in range(4))
    r = _dot(mix(0), wr_ref[...])
    k = _dot(mix(2), wk_ref[...])
    v = _dot(mix(3), wv_ref[...])
    lw = _dot(jnp.tanh(_dot(mix(1), w1_ref[...])).astype(BF16), w2_ref[...])
    w_log = -_softplus(-(w0 + lw)) - 0.5
    decay = jnp.exp(-jnp.exp(w_log))
    a = jax.nn.sigmoid(a0 + _dot(_dot(mix(4), a1_ref[...]).astype(BF16), a2_ref[...]))
    g = _dot(jax.nn.sigmoid(_dot(mix(5), g1_ref[...])).astype(BF16), g2_ref[...])
    kk = k * k_k
    ss = _split_dot(_tile_sum(kk * kk), gmat_ref[...], 3)
    kk = kk * _tile8(lax.rsqrt(jnp.maximum(ss, 1e-24)))
    r_ref[...] = r
    w_ref[...] = decay
    k_ref[...] = k * (1.0 + (a - 1.0) * k_a)
    v_ref[...] = v
    kk_ref[...] = kk
    kka_ref[...] = kk * a
    g_ref[...] = g


def _rwkv_proj(h, hp, mu, wr, wk, wv, w1, w2, a1, a2, g1, g2, vecs, gmat, tm=256):
    t = h.shape[0]
    row = lambda i: (i, 0)
    fixed = lambda i: (0, 0)
    full = lambda a: pl.BlockSpec(a.shape, fixed)
    tok = pl.BlockSpec((tm, D_MODEL), row)
    consts = [mu, wr, wk, wv, w1, w2, a1, a2, g1, g2, vecs, gmat]
    return pl.pallas_call(
        _rwkv_proj_kernel, grid=(t // tm,), in_specs=[tok, tok] + [full(a) for a in consts],
        out_specs=[tok] * 7, out_shape=[jax.ShapeDtypeStruct((t, D_MODEL), F32)] * 7,
        compiler_params=_params("parallel"), name="rwkv_proj")(h, hp, *consts)


def _rwkv_scan_kernel(*refs, nb, tc):
    ins = refs[:6 * nb]
    r_refs, w_refs, k_refs, v_refs, kk_refs, kka_refs = (ins[i * nb:(i + 1) * nb] for i in range(6))
    s0_ref, gmat_ref, y_ref, sout_ref, s_s = refs[6 * nb:]
    tt = pl.program_id(1)
    groups = HEAD_DIM // SUBLANES
    eye = (lax.broadcasted_iota(jnp.int32, (SUBLANES, LANES), 1) // N_HEADS
           == lax.broadcasted_iota(jnp.int32, (SUBLANES, LANES), 0))
    gmat = gmat_ref[...]

    @pl.when(tt == 0)
    def _():
        s_s[...] = s0_ref[...]

    def bcast(ref, t):
        return jnp.broadcast_to(ref[pl.ds(t, 1), :], (SUBLANES, D_MODEL))

    def step(t, _):
        kk = [bcast(kk_refs[b], t) for b in range(nb)]
        sa_parts = []
        v_parts = []
        for b in range(nb):
            vrow = bcast(v_refs[b], t)
            for g in range(groups):
                sa_parts.append(_tile_sum(s_s[b, g * SUBLANES:(g + 1) * SUBLANES, :] * kk[b]))
                v_parts.append(jnp.where(eye, vrow[:, g * LANES:(g + 1) * LANES], 0.0))
        sa_all = _split_dot(jnp.concatenate(sa_parts, axis=0), gmat, 2)
        v_all = _split_dot(jnp.concatenate(v_parts, axis=0), gmat, 2)
        y_parts = []
        for b in range(nb):
            w = bcast(w_refs[b], t)
            kka = bcast(kka_refs[b], t)
            k = bcast(k_refs[b], t)
            r = bcast(r_refs[b], t)
            for g in range(groups):
                lo = (b * groups + g) * SUBLANES
                sa = _tile8(-sa_all[lo:lo + SUBLANES, :])
                vv = _tile8(v_all[lo:lo + SUBLANES, :])
                s_new = s_s[b, g * SUBLANES:(g + 1) * SUBLANES, :] * w + sa * kka + vv * k
                s_s[b, g * SUBLANES:(g + 1) * SUBLANES, :] = s_new
                y_parts.append(_tile_sum(s_new * r))
        y_all = _split_dot(jnp.concatenate(y_parts, axis=0), gmat, 2)
        t8 = pl.multiple_of((t // SUBLANES) * SUBLANES, SUBLANES)
        mine = lax.broadcasted_iota(jnp.int32, (SUBLANES, LANES), 0) == t % SUBLANES
        for b in range(nb):
            for g in range(groups):
                lo = (b * groups + g) * SUBLANES
                yrow = jnp.sum(jnp.where(eye, y_all[lo:lo + SUBLANES, :], 0.0), axis=0, keepdims=True)
                cur = y_ref[b, pl.ds(t8, SUBLANES), g * LANES:(g + 1) * LANES]
                y_ref[b, pl.ds(t8, SUBLANES), g * LANES:(g + 1) * LANES] = jnp.where(
                    mine, jnp.broadcast_to(yrow, (SUBLANES, LANES)), cur)
        return 0

    y_ref[...] = jnp.zeros_like(y_ref)
    lax.fori_loop(0, tc, step, 0)

    @pl.when(tt == pl.num_programs(1) - 1)
    def _():
        sout_ref[...] = s_s[...]


def _rwkv_scan(vecs, s0, gmat, batch, seq, row0, nb, tc):
    nt = seq // tc
    blk0 = row0 // tc

    def tok(n):
        return pl.BlockSpec((tc, D_MODEL), lambda bb, tt: (blk0 + (bb * nb + n) * nt + tt, 0))

    in_specs = [tok(n) for _ in range(6) for n in range(nb)]
    args = [a for a in vecs for _ in range(nb)]
    state = pl.BlockSpec((nb, HEAD_DIM, D_MODEL), lambda bb, tt: (bb, 0, 0))
    in_specs += [state, pl.BlockSpec((LANES, LANES), lambda bb, tt: (0, 0))]
    return pl.pallas_call(
        functools.partial(_rwkv_scan_kernel, nb=nb, tc=tc), grid=(batch // nb, nt), in_specs=in_specs,
        out_specs=[pl.BlockSpec((nb, tc, D_MODEL), lambda bb, tt: (bb, tt, 0)), state],
        out_shape=[jax.ShapeDtypeStruct((batch, seq, D_MODEL), F32),
                   jax.ShapeDtypeStruct((batch, HEAD_DIM, D_MODEL), F32)],
        scratch_shapes=[pltpu.VMEM((nb, HEAD_DIM, D_MODEL), F32)],
        compiler_params=_params("parallel", "arbitrary"), name="rwkv_scan")(*args, s0, gmat)


def _rwkv_out_kernel(yp_ref, ys_ref, r_ref, k_ref, v_ref, g_ref, x_ref, vec_ref, gmat_ref, wo_ref, gpost_ref, o_ref, *,
                     prompt_tiles):
    r_k, ln_w, ln_b = (vec_ref[i:i + 1, :] for i in range(3))
    gmat = gmat_ref[...]
    y = jnp.where(pl.program_id(0) < prompt_tiles, yp_ref[...], ys_ref[...])
    mean = _split_dot(_tile_sum(y), gmat, 3) * (1.0 / HEAD_DIM)
    yc = y - _tile8(mean)
    var = _split_dot(_tile_sum(yc * yc), gmat, 3) * (1.0 / HEAD_DIM)
    yn = yc * _tile8(lax.rsqrt(var + GN_EPS)) * ln_w + ln_b
    bonus = _split_dot(_tile_sum(r_ref[...] * k_ref[...] * r_k), gmat, 3)
    out = (yn + _tile8(bonus) * v_ref[...]) * g_ref[...]
    m = _dot(out.astype(BF16), wo_ref[...])
    o_ref[...] = x_ref[...] + _rms(m, gpost_ref[...])


def _rwkv_out(y_p, y_s, r, k, v, g, x, vecs, gmat, wo, gpost, tm=256):
    t = x.shape[0]
    n_p = y_p.shape[0] // tm
    row = lambda i: (i, 0)
    fixed = lambda i: (0, 0)
    tok = pl.BlockSpec((tm, D_MODEL), row)
    consts = [vecs, gmat, wo, gpost]
    y_specs = [pl.BlockSpec((tm, D_MODEL), lambda i: (jnp.minimum(i, n_p - 1), 0)),
               pl.BlockSpec((tm, D_MODEL), lambda i: (jnp.maximum(i - n_p, 0), 0))]
    return pl.pallas_call(
        functools.partial(_rwkv_out_kernel, prompt_tiles=n_p), grid=(t // tm,),
        in_specs=y_specs + [tok] * 5 + [pl.BlockSpec(a.shape, fixed) for a in consts],
        out_specs=tok, out_shape=jax.ShapeDtypeStruct((t, D_MODEL), F32),
        compiler_params=_params("parallel"), name="rwkv_out")(y_p, y_s, r, k, v, g, x, *consts)


def _block_diag_q(q_perm, db, ds):
    inv = np.argsort(np.asarray(HEAD_PERM))
    q = q_perm.reshape(db, ds, N_HEADS, HEAD_DIM)[:, :, inv]
    q = q.reshape(db, ds, KV_HEADS, N_HEADS // KV_HEADS, HEAD_DIM).transpose(0, 2, 3, 1, 4)
    eye = jnp.eye(KV_HEADS, dtype=q.dtype)
    qbd = q[:, :, :, :, None, :] * eye[None, :, None, None, :, None]
    return qbd.reshape(db, N_HEADS * ds, D_KV)


def _undo_block_diag(o, db, ds):
    o = o.reshape(db, KV_HEADS, N_HEADS // KV_HEADS, ds, KV_HEADS, HEAD_DIM)
    o = jnp.stack([o[:, kv, :, :, kv, :] for kv in range(KV_HEADS)], axis=1)
    o = o.transpose(0, 3, 1, 2, 4).reshape(db, ds, N_HEADS, HEAD_DIM)
    return o[:, :, np.asarray(HEAD_PERM)].reshape(db * ds, D_MODEL)


def _pad_page(x, db, ds):
    return jnp.pad(x.reshape(db, ds, -1), ((0, 0), (0, PAGE - ds), (0, 0)))


def _rope_tables(pos):
    inv_freq = ROPE_THETA ** (-jnp.arange(0, ROT_DIM, 2, dtype=F32) / ROT_DIM)
    ang = pos.astype(F32)[:, None] * inv_freq[None, :]
    cos, sin = jnp.cos(ang), jnp.sin(ang)
    half = ROT_DIM // 2
    ones = jnp.ones((pos.shape[0], HEAD_DIM - ROT_DIM), F32)
    zeros = jnp.zeros((pos.shape[0], HEAD_DIM - ROT_DIM), F32)
    zh = jnp.zeros_like(sin)
    a = jnp.concatenate([cos, cos, ones], axis=1)
    m = jnp.concatenate([-sin, zh, zeros], axis=1)
    p = jnp.concatenate([zh, sin, zeros], axis=1)
    rep = LANES // HEAD_DIM
    return tuple(jnp.tile(z, (1, rep)) for z in (a, m, p))


def kernel(x_prompt, x_sample, cache_fox_k_l0, cache_fox_v_l0, cache_fox_logf_l0, state_rwkv_wkv_l1, state_rwkv_shift_l1, cache_swa_k_l2, cache_swa_v_l2, cache_fox_k_l3, cache_fox_v_l3, cache_fox_logf_l3, page_table, p_prompt, p_sample, norm_mix_pre, norm_mix_post, norm_ffn_pre, norm_ffn_post, ffn_w_up, ffn_w_down, ple_w_proj, ple_w_gate, fox_w_q, fox_w_k, fox_w_v, fox_w_f, fox_b_f, fox_w_o, rwkv_mu, rwkv_w_r, rwkv_w_k, rwkv_w_v, rwkv_w_o, rwkv_w0, rwkv_w1, rwkv_w2, rwkv_a0, rwkv_a1, rwkv_a2, rwkv_g1, rwkv_g2, rwkv_k_k, rwkv_k_a, rwkv_r_k, rwkv_ln_w, rwkv_ln_b, swa_w_q, swa_b_q, swa_w_k, swa_b_k, swa_w_v, swa_b_v, swa_sinks, swa_w_o):
    bp, sp, _ = x_prompt.shape
    db, ds, _ = x_sample.shape
    depth = norm_mix_pre.shape[0]
    tp, ts = bp * sp, db * ds
    n_pages = page_table.shape[1]
    past = n_pages * PAGE
    bf = lambda a: a.astype(BF16)
    vec = lambda a: a.reshape(1, -1).astype(F32)

    x = jnp.concatenate([x_prompt.reshape(tp, D_MODEL), x_sample.reshape(ts, D_MODEL)], axis=0)
    pp_all = p_prompt.reshape(depth * tp, D_PLE)
    ps_all = p_sample.reshape(depth * ts, D_PLE)
    pos = jnp.concatenate([jnp.tile(jnp.arange(sp), bp), jnp.tile(past + jnp.arange(ds), db)])
    gmat = (np.arange(LANES)[:, None] % N_HEADS == np.arange(LANES)[None, :] % N_HEADS)
    gmat = jnp.asarray(gmat, BF16)
    fox_caches = ((cache_fox_k_l0, cache_fox_v_l0, cache_fox_logf_l0), (cache_fox_k_l3, cache_fox_v_l3, cache_fox_logf_l3))
    new_state = []

    for i in range(depth):
        kind, j = i % N_MIXERS, i // N_MIXERS
        g_pre = vec(norm_mix_pre[i])
        g_post = vec(norm_mix_post[i])
        if kind == 0:
            wf = jnp.pad(fox_w_f[j], ((0, 0), (0, LANES - N_HEADS)))
            w = bf(jnp.concatenate([fox_w_q[j][:, Q_COLS], fox_w_k[j], fox_w_v[j], wf], axis=1))
            b = jnp.concatenate([jnp.zeros((D_MODEL + 2 * D_KV,), F32), fox_b_f[j], jnp.zeros((LANES - N_HEADS,), F32)])
            q, k, v, lf = _proj(x, g_pre, w, vec(b), None)
            lf = lf[:, :N_HEADS]
            lf_p = lf[:tp].reshape(bp, sp, N_HEADS)
            ck = _cumsum(lf_p.transpose(0, 2, 1))
            cq = ck.transpose(0, 2, 1).reshape(tp, N_HEADS)
            o_p = _flash(q, k, v, cq, bp, sp)
            ck_cache, cv_cache, clf_cache = fox_caches[j]
            n_pool = ck_cache.shape[0]
            pages_t = lambda c: c.transpose(0, 2, 3, 1).reshape(n_pool, D_KV, PAGE)
            new_t = lambda z: _pad_page(z, db, ds).transpose(0, 2, 1)
            o_s = _fox_sample(
                page_table, _block_diag_q(q[tp:], db, ds), new_t(k[tp:]), new_t(v[tp:]), new_t(lf[tp:]),
                pages_t(ck_cache), pages_t(cv_cache), clf_cache.transpose(0, 2, 1))
            x = _oproj(o_p, _undo_block_diag(o_s, db, ds), bf(fox_w_o[j][Q_COLS, :]), g_post, x)
            new_state.append((k[:tp].reshape(bp, sp, KV_HEADS, HEAD_DIM), v[:tp].reshape(bp, sp, KV_HEADS, HEAD_DIM),
                              lf_p, k[tp:].reshape(db, ds, KV_HEADS, HEAD_DIM),
                              v[tp:].reshape(db, ds, KV_HEADS, HEAD_DIM), lf[tp:].reshape(db, ds, N_HEADS)))
        elif kind == 1:
            h = _norm(x, g_pre)
            h_p = h[:tp].reshape(bp, sp, D_MODEL)
            h_s = h[tp:].reshape(db, ds, D_MODEL)
            hp = jnp.concatenate([
                jnp.concatenate([jnp.zeros((bp, 1, D_MODEL), F32), h_p[:, :-1]], axis=1).reshape(tp, D_MODEL),
                jnp.concatenate([state_rwkv_shift_l1[:, None, :], h_s[:, :-1]], axis=1).reshape(ts, D_MODEL)], axis=0)
            pc = lambda a: a[:, R_COLS]
            vecs = jnp.stack([rwkv_w0[j][R_COLS], rwkv_a0[j][R_COLS], rwkv_k_k[j][R_COLS], rwkv_k_a[j][R_COLS]])
            r, w, k, v, kk, kka, g = _rwkv_proj(
                h, hp, rwkv_mu[j], bf(pc(rwkv_w_r[j])), bf(pc(rwkv_w_k[j])), bf(pc(rwkv_w_v[j])), bf(rwkv_w1[j]),
                bf(pc(rwkv_w2[j])), bf(rwkv_a1[j]), bf(pc(rwkv_a2[j])), bf(rwkv_g1[j]), bf(pc(rwkv_g2[j])), vecs, gmat)
            scan_in = (r, w, k, v, kk, kka)
            to_lanes = lambda s: s.transpose(0, 2, 3, 1).reshape(s.shape[0], HEAD_DIM, D_MODEL)
            from_lanes = lambda s: s.reshape(s.shape[0], HEAD_DIM, HEAD_DIM, N_HEADS).transpose(0, 3, 1, 2)
            y_p, s_p = _rwkv_scan(scan_in, jnp.zeros((bp, HEAD_DIM, D_MODEL), F32), gmat, bp, sp, 0, bp, 64)
            y_s, s_s = _rwkv_scan(scan_in, to_lanes(state_rwkv_wkv_l1.astype(F32)), gmat, db, ds, tp, 4, ds)
            ovecs = jnp.stack([rwkv_r_k[j].reshape(-1)[R_COLS], rwkv_ln_w[j][R_COLS], rwkv_ln_b[j][R_COLS]])
            x = _rwkv_out(y_p.reshape(tp, D_MODEL), y_s.reshape(ts, D_MODEL), r, k, v, g, x, ovecs, gmat,
                          bf(rwkv_w_o[j][R_COLS, :]), g_post)
            new_state.append((from_lanes(s_p), h_p[:, -1], from_lanes(s_s), h_s[:, -1]))
        else:
            w = bf(jnp.concatenate([swa_w_q[j][:, Q_COLS], swa_w_k[j], swa_w_v[j]], axis=1))
            b = jnp.concatenate([swa_b_q[j][Q_COLS], swa_b_k[j], swa_b_v[j]])
            q, k, v = _proj(x, g_pre, w, vec(b), _rope_tables(pos))
            o_p = _swa_band(q, k, v, vec(swa_sinks[j]), bp, sp)
            keep = cache_swa_k_l2.shape[1]
            k_s = k[tp:].reshape(db, ds, D_KV)
            v_s = v[tp:].reshape(db, ds, D_KV)
            kbuf = cache_swa_k_l2.reshape(db, keep, D_KV)
            vbuf = cache_swa_v_l2.reshape(db, keep, D_KV)
            sink_rows = jnp.repeat(swa_sinks[j].astype(F32), ds).reshape(N_HEADS * ds, 1)
            o_s = _swa_sample(_block_diag_q(q[tp:], db, ds), kbuf, vbuf, _pad_page(k[tp:], db, ds),
                              _pad_page(v[tp:], db, ds), sink_rows)
            x = _oproj(o_p, _undo_block_diag(o_s, db, ds), bf(swa_w_o[j][Q_COLS, :]), g_post, x)
            k_p = k[:tp].reshape(bp, sp, KV_HEADS, HEAD_DIM)
            v_p = v[:tp].reshape(bp, sp, KV_HEADS, HEAD_DIM)
            wk = min(WINDOW, sp)
            new_state.append((k_p[:, sp - wk:], v_p[:, sp - wk:],
                              jnp.concatenate([kbuf, k_s], axis=1)[:, ds:].reshape(db, keep, KV_HEADS, HEAD_DIM),
                              jnp.concatenate([vbuf, v_s], axis=1)[:, ds:].reshape(db, keep, KV_HEADS, HEAD_DIM)))
        x = _ffn(x, vec(norm_ffn_pre[i]), bf(ffn_w_up[i]), bf(ffn_w_down[i]), vec(norm_ffn_post[i]), pp_all, ps_all, i,
                 tp, bf(ple_w_proj[i]), bf(ple_w_gate[i]))

    outs = [x[:tp].reshape(bp, sp, D_MODEL), x[tp:].reshape(db, ds, D_MODEL)]
    for st in new_state:
        outs.extend(st)
    return tuple(outs)
```

```python
import functools

import numpy as np
import jax
import jax.numpy as jnp
from jax import lax
from jax.experimental import pallas as pl
from jax.experimental.pallas import tpu as pltpu

F32 = jnp.float32
BF16 = jnp.bfloat16

D_MODEL = 1024
HEAD_DIM = 64
N_HEADS = 16
KV_HEADS = 4
D_KV = KV_HEADS * HEAD_DIM
D_FF = 4096
D_PLE = 256
PAGE = 128
WINDOW = 128
ROT_DIM = 16
ROPE_THETA = 500000.0
NORM_EPS = 1e-6
GN_EPS = 64e-5
N_MIXERS = 3
LANES = 128
SUBLANES = 8
NEG = -1e30
VMEM_LIMIT = 48 * 1024 * 1024
SCALE = HEAD_DIM ** -0.5
C_PIECES = 3

HEAD_PERM = (0, 4, 1, 5, 2, 6, 3, 7, 8, 12, 9, 13, 10, 14, 11, 15)
Q_COLS = np.concatenate([np.arange(HEAD_DIM) + HEAD_DIM * h for h in HEAD_PERM])
R_COLS = (np.arange(D_MODEL) % N_HEADS) * HEAD_DIM + np.arange(D_MODEL) // N_HEADS


def _params(*sem):
    return pltpu.CompilerParams(dimension_semantics=sem, vmem_limit_bytes=VMEM_LIMIT)


def _dot(a, b):
    return jnp.dot(a, b, preferred_element_type=F32)


def _dot_nt(a, b):
    return lax.dot_general(a, b, (((1,), (1,)), ((), ())), preferred_element_type=F32)


def _rms(x, g):
    return x * lax.rsqrt(jnp.mean(x * x, axis=-1, keepdims=True) + NORM_EPS) * g


def _log_sigmoid(x):
    return jnp.minimum(x, 0.0) - jnp.log1p(jnp.exp(-jnp.abs(x)))


def _softplus(x):
    return jnp.maximum(x, 0.0) + jnp.log1p(jnp.exp(-jnp.abs(x)))


def _split_dot(x, g, parts):
    out = None
    for _ in range(parts):
        piece = x.astype(BF16)
        term = _dot(piece, g)
        out = term if out is None else out + term
        x = x - piece.astype(F32)
    return out


def _tile_sum(x):
    out = x[:, 0:LANES]
    for t in range(1, D_MODEL // LANES):
        out = out + x[:, t * LANES:(t + 1) * LANES]
    return out


def _tile8(x):
    return jnp.concatenate([x] * (D_MODEL // LANES), axis=1)


def _rope(x, a, m, p):
    chunks = []
    for c in range(x.shape[1] // LANES):
        xc = x[:, c * LANES:(c + 1) * LANES]
        chunks.append(xc * a + pltpu.roll(xc, LANES - ROT_DIM // 2, 1) * m + pltpu.roll(xc, ROT_DIM // 2, 1) * p)
    return jnp.concatenate(chunks, axis=1)


def _proj_kernel(x_ref, g_ref, w_ref, b_ref, *rest, rope):
    if rope:
        ra_ref, rm_ref, rp_ref, q_ref, k_ref, v_ref = rest
    else:
        q_ref, k_ref, v_ref, lf_ref = rest
    h = _rms(x_ref[...], g_ref[...]).astype(BF16)
    res = _dot(h, w_ref[...]) + b_ref[...]
    q = res[:, :D_MODEL]
    k = res[:, D_MODEL:D_MODEL + D_KV]
    v = res[:, D_MODEL + D_KV:D_MODEL + 2 * D_KV]
    if rope:
        a, m, p = ra_ref[...], rm_ref[...], rp_ref[...]
        q = _rope(q, a, m, p)
        k = _rope(k, a, m, p)
    else:
        lf_ref[...] = _log_sigmoid(res[:, D_MODEL + 2 * D_KV:])
    q_ref[...] = (q * SCALE).astype(BF16)
    k_ref[...] = k
    v_ref[...] = v


def _proj(x, g, w, b, rope_tabs, tm=512):
    t = x.shape[0]
    n = w.shape[1]
    rope = rope_tabs is not None
    row = lambda i: (i, 0)
    fixed = lambda i: (0, 0)
    in_specs = [pl.BlockSpec((tm, D_MODEL), row), pl.BlockSpec((1, D_MODEL), fixed),
                pl.BlockSpec((D_MODEL, n), fixed), pl.BlockSpec((1, n), fixed)]
    args = [x, g, w, b]
    out_shape = [jax.ShapeDtypeStruct((t, D_MODEL), BF16), jax.ShapeDtypeStruct((t, D_KV), F32),
                 jax.ShapeDtypeStruct((t, D_KV), F32)]
    out_specs = [pl.BlockSpec((tm, D_MODEL), row), pl.BlockSpec((tm, D_KV), row), pl.BlockSpec((tm, D_KV), row)]
    if rope:
        in_specs += [pl.BlockSpec((tm, LANES), row)] * 3
        args += list(rope_tabs)
    else:
        out_shape.append(jax.ShapeDtypeStruct((t, LANES), F32))
        out_specs.append(pl.BlockSpec((tm, LANES), row))
    return pl.pallas_call(
        functools.partial(_proj_kernel, rope=rope), grid=(t // tm,), in_specs=in_specs, out_specs=out_specs,
        out_shape=out_shape, compiler_params=_params("parallel"),
        name="swa_proj" if rope else "fox_proj")(*args)


def _oproj_kernel(op_ref, os_ref, w_ref, g_ref, x_ref, y_ref, *, prompt_tiles):
    o = jnp.where(pl.program_id(0) < prompt_tiles, op_ref[...], os_ref[...])
    m = _dot(o, w_ref[...])
    y_ref[...] = x_ref[...] + _rms(m, g_ref[...])


def _oproj(o_p, o_s, w, g, x, tm=512):
    t = x.shape[0]
    n_p = o_p.shape[0] // tm
    row = lambda i: (i, 0)
    fixed = lambda i: (0, 0)
    return pl.pallas_call(
        functools.partial(_oproj_kernel, prompt_tiles=n_p), grid=(t // tm,),
        in_specs=[pl.BlockSpec((tm, D_MODEL), lambda i: (jnp.minimum(i, n_p - 1), 0)),
                  pl.BlockSpec((tm, D_MODEL), lambda i: (jnp.maximum(i - n_p, 0), 0)),
                  pl.BlockSpec((D_MODEL, D_MODEL), fixed), pl.BlockSpec((1, D_MODEL), fixed),
                  pl.BlockSpec((tm, D_MODEL), row)],
        out_specs=pl.BlockSpec((tm, D_MODEL), row), out_shape=jax.ShapeDtypeStruct((t, D_MODEL), F32),
        compiler_params=_params("parallel"), name="out_proj")(o_p, o_s, w, g, x)


def _ffn_kernel(x_ref, gpre_ref, wup_ref, wdn_ref, gpost_ref, pp_ref, ps_ref, wp_ref, wg_ref, y_ref, *, prompt_tiles, tf):
    x = x_ref[...]
    h = _rms(x, gpre_ref[...]).astype(BF16)
    acc = None
    for c in range(D_FF // tf):
        u = jnp.maximum(_dot(h, wup_ref[:, c * tf:(c + 1) * tf]), 0.0)
        part = _dot((u * u).astype(BF16), wdn_ref[c * tf:(c + 1) * tf, :])
        acc = part if acc is None else acc + part
    x1 = x + _rms(acc, gpost_ref[...])
    gate = jax.nn.sigmoid(_dot(x1.astype(BF16), wg_ref[...]))
    p = jnp.where(pl.program_id(0) < prompt_tiles, pp_ref[...], ps_ref[...])
    y_ref[...] = x1 + _dot(p.astype(BF16), wp_ref[...]) * gate


def _ffn(x, gpre, wup, wdn, gpost, p_prompt, p_sample, layer, tp, wp, wg, tm=512, tf=512):
    t = x.shape[0]
    n_p = tp // tm
    n_s = (t - tp) // tm
    row = lambda i: (i, 0)
    fixed = lambda i: (0, 0)
    once = dict(pipeline_mode=pl.Buffered(1))
    return pl.pallas_call(
        functools.partial(_ffn_kernel, prompt_tiles=n_p, tf=tf), grid=(t // tm,),
        in_specs=[pl.BlockSpec((tm, D_MODEL), row), pl.BlockSpec((1, D_MODEL), fixed),
                  pl.BlockSpec((D_MODEL, D_FF), fixed, **once), pl.BlockSpec((D_FF, D_MODEL), fixed, **once),
                  pl.BlockSpec((1, D_MODEL), fixed),
                  pl.BlockSpec((tm, D_PLE), lambda i: (layer * n_p + jnp.minimum(i, n_p - 1), 0)),
                  pl.BlockSpec((tm, D_PLE), lambda i: (layer * n_s + jnp.maximum(i - n_p, 0), 0)),
                  pl.BlockSpec((D_PLE, D_MODEL), fixed, **once), pl.BlockSpec((D_MODEL, D_MODEL), fixed, **once)],
        out_specs=pl.BlockSpec((tm, D_MODEL), row), out_shape=jax.ShapeDtypeStruct((t, D_MODEL), F32),
        compiler_params=_params("parallel"), name="ffn_ple")(x, gpre, wup, wdn, gpost, p_prompt, p_sample, wp, wg)


def _cumsum_kernel(x_ref, c_ref):
    n = x_ref.shape[1]
    lane = lax.broadcasted_iota(jnp.int32, (N_HEADS, LANES), 1)
    carry = jnp.zeros((N_HEADS, 1), F32)
    for c in range(n // LANES):
        blk = x_ref[:, c * LANES:(c + 1) * LANES]
        sh = 1
        while sh < LANES:
            blk = blk + jnp.where(lane >= sh, pltpu.roll(blk, sh, 1), 0.0)
            sh *= 2
        blk = blk + carry
        c_ref[:, c * LANES:(c + 1) * LANES] = blk
        carry = blk[:, LANES - 1:LANES]


def _cumsum(x):
    b, h, t = x.shape
    spec = pl.BlockSpec((None, h, t), lambda i: (i, 0, 0))
    return pl.pallas_call(_cumsum_kernel, grid=(b,), in_specs=[spec], out_specs=spec,
                          out_shape=jax.ShapeDtypeStruct(x.shape, F32), compiler_params=_params("parallel"),
                          name="logf_cumsum")(x)


def _flash_kernel(qi_tab, ki_tab, first_tab, last_tab, q_ref, k_ref, v_ref, cq_ref, crep_ref, o_ref, qm_s, m_s, l_s, acc_s,
                  *, tq, tk):
    step = pl.program_id(1)
    qi = qi_tab[step]
    ki = ki_tab[step]
    lane = lax.broadcasted_iota(jnp.int32, (tq, LANES), 1)
    rep = tk // LANES

    @pl.when(first_tab[step] == 1)
    def _():
        m_s[...] = jnp.full_like(m_s, NEG)
        l_s[...] = jnp.zeros_like(l_s)
        acc_s[...] = jnp.zeros_like(acc_s)
        for pp in range(N_HEADS):
            h = HEAD_PERM[pp]
            qp = q_ref[:, (pp // 2) * LANES:(pp // 2 + 1) * LANES]
            keep = (lane < HEAD_DIM) if pp % 2 == 0 else (lane >= HEAD_DIM)
            qm_s[pp * tq:(pp + 1) * tq, 0:LANES] = jnp.where(keep, qp, jnp.zeros_like(qp))
            ones = (lane % N_HEADS == h) & (lane < C_PIECES * N_HEADS)
            qm_s[pp * tq:(pp + 1) * tq, LANES:2 * LANES] = jnp.where(ones, 1.0, 0.0).astype(BF16)

    kb = k_ref[...].astype(BF16)
    vb = v_ref[...].astype(BF16)
    klane = lax.broadcasted_iota(jnp.int32, (tk, LANES), 1)
    rest = -crep_ref[...]
    cneg = jnp.zeros((tk, LANES), BF16)
    for p in range(C_PIECES):
        piece = rest.astype(BF16)
        cneg = jnp.where((klane >= p * N_HEADS) & (klane < (p + 1) * N_HEADS), piece, cneg)
        rest = rest - piece.astype(F32)
    row = qi * tq + lax.broadcasted_iota(jnp.int32, (tq, tk), 0)
    col = ki * tk + lax.broadcasted_iota(jnp.int32, (tq, tk), 1)
    group = N_HEADS // 2
    mask = jnp.concatenate([col <= row] * group, axis=0)
    for g0 in range(0, N_HEADS, group):
        kv = g0 // group
        kp = jnp.concatenate([kb[:, kv * LANES:(kv + 1) * LANES], cneg], axis=1)
        vp = vb[:, kv * LANES:(kv + 1) * LANES]
        rows = slice(g0 * tq, (g0 + group) * tq)
        s = jnp.where(mask, _dot_nt(qm_s[rows, :], kp), NEG)
        cq = jnp.concatenate([jnp.broadcast_to(cq_ref[:, HEAD_PERM[g0 + i]:HEAD_PERM[g0 + i] + 1], (tq, LANES))
                              for i in range(group)], axis=0)
        m_old = m_s[rows, :]
        m_new = jnp.maximum(m_old, jnp.max(s, axis=1, keepdims=True) + cq)
        alpha = jnp.exp(m_old - m_new)
        p = jnp.exp(s - jnp.concatenate([m_new - cq] * rep, axis=1))
        l_s[rows, :] = alpha * l_s[rows, :] + jnp.sum(p, axis=1, keepdims=True)
        m_s[rows, :] = m_new
        pv = _dot(p.astype(BF16), vp)
        for i in range(0, group, 2):
            pair = (g0 + i) // 2
            a = acc_s[:, pair * LANES:(pair + 1) * LANES]
            a0, a1 = alpha[i * tq:(i + 1) * tq, :], alpha[(i + 1) * tq:(i + 2) * tq, :]
            acc_s[:, pair * LANES:(pair + 1) * LANES] = jnp.where(
                lane < HEAD_DIM, a0 * a + pv[i * tq:(i + 1) * tq, :], a1 * a + pv[(i + 1) * tq:(i + 2) * tq, :])

    @pl.when(last_tab[step] == 1)
    def _():
        for pair in range(N_HEADS // 2):
            l0 = l_s[(2 * pair) * tq:(2 * pair + 1) * tq, :]
            l1 = l_s[(2 * pair + 1) * tq:(2 * pair + 2) * tq, :]
            inv = jnp.where(lane < HEAD_DIM, 1.0 / l0, 1.0 / l1)
            o_ref[:, pair * LANES:(pair + 1) * LANES] = (acc_s[:, pair * LANES:(pair + 1) * LANES] * inv).astype(BF16)


def _flash(q, k, v, cq, batch, seq, tq=512):
    crep = jnp.concatenate([cq] * C_PIECES + [jnp.zeros((cq.shape[0], LANES - C_PIECES * N_HEADS), F32)], axis=1)
    tk = tq
    nq = seq // tq
    pairs = [(a, b) for a in range(nq) for b in range(a + 1)]
    qi_tab = jnp.asarray([a for a, _ in pairs], jnp.int32)
    ki_tab = jnp.asarray([b for _, b in pairs], jnp.int32)
    first_tab = jnp.asarray([int(b == 0) for a, b in pairs], jnp.int32)
    last_tab = jnp.asarray([int(b == a) for a, b in pairs], jnp.int32)
    qmap = lambda b, s, qt, kt, ft, lt: (b * nq + qt[s], 0)
    kmap = lambda b, s, qt, kt, ft, lt: (b * nq + kt[s], 0)
    in_specs = [pl.BlockSpec((tq, D_MODEL), qmap), pl.BlockSpec((tk, D_KV), kmap), pl.BlockSpec((tk, D_KV), kmap),
                pl.BlockSpec((tq, N_HEADS), qmap), pl.BlockSpec((tk, LANES), kmap)]
    grid_spec = pltpu.PrefetchScalarGridSpec(
        num_scalar_prefetch=4, grid=(batch, len(pairs)), in_specs=in_specs,
        out_specs=pl.BlockSpec((tq, D_MODEL), qmap),
        scratch_shapes=[pltpu.VMEM((N_HEADS * tq, 2 * LANES), BF16), pltpu.VMEM((N_HEADS * tq, LANES), F32),
                        pltpu.VMEM((N_HEADS * tq, LANES), F32), pltpu.VMEM((tq, D_MODEL), F32)])
    return pl.pallas_call(
        functools.partial(_flash_kernel, tq=tq, tk=tk), grid_spec=grid_spec,
        out_shape=jax.ShapeDtypeStruct((batch * seq, D_MODEL), BF16),
        compiler_params=_params("parallel", "arbitrary"),
        name="fox_prompt_attn")(qi_tab, ki_tab, first_tab, last_tab, q, k, v, cq, crep)


def _swa_band_kernel(q_ref, kp_ref, ko_ref, vp_ref, vo_ref, sink_ref, o_ref, *, tq):
    qi = pl.program_id(1)
    nk = WINDOW + tq
    kb = jnp.concatenate([kp_ref[...], ko_ref[...]], axis=0).astype(BF16)
    vb = jnp.concatenate([vp_ref[...], vo_ref[...]], axis=0).astype(BF16)
    lane = lax.broadcasted_iota(jnp.int32, (tq, LANES), 1)
    a = lax.broadcasted_iota(jnp.int32, (tq, nk), 0)
    c = lax.broadcasted_iota(jnp.int32, (tq, nk), 1)
    mask = (c > a) & (c <= a + WINDOW) & (qi * tq + c >= WINDOW)
    for pair in range(N_HEADS // 2):
        kv = pair // 4
        kp = kb[:, kv * LANES:(kv + 1) * LANES]
        vp = vb[:, kv * LANES:(kv + 1) * LANES]
        qp = q_ref[:, pair * LANES:(pair + 1) * LANES]
        outs = []
        for half in range(2):
            h = HEAD_PERM[2 * pair + half]
            keep = (lane < HEAD_DIM) if half == 0 else (lane >= HEAD_DIM)
            s = _dot_nt(jnp.where(keep, qp, jnp.zeros_like(qp)), kp)
            s = jnp.where(mask, s, NEG)
            sink = sink_ref[:, h:h + 1]
            m = jnp.maximum(jnp.max(s, axis=1, keepdims=True), sink)
            p = jnp.exp(s - m)
            l = jnp.sum(p, axis=1, keepdims=True) + jnp.exp(sink - m)
            outs.append(_dot(p.astype(BF16), vp) * (1.0 / l))
        o_ref[:, pair * LANES:(pair + 1) * LANES] = jnp.where(lane < HEAD_DIM, outs[0], outs[1]).astype(BF16)


def _swa_band(q, k, v, sinks, batch, seq, tq=256):
    nq = seq // tq
    r = tq // WINDOW
    qmap = lambda b, i: (b * nq + i, 0)
    pmap = lambda b, i: (jnp.maximum((b * nq + i) * r - 1, 0), 0)
    return pl.pallas_call(
        functools.partial(_swa_band_kernel, tq=tq), grid=(batch, nq),
        in_specs=[pl.BlockSpec((tq, D_MODEL), qmap), pl.BlockSpec((WINDOW, D_KV), pmap), pl.BlockSpec((tq, D_KV), qmap),
                  pl.BlockSpec((WINDOW, D_KV), pmap), pl.BlockSpec((tq, D_KV), qmap),
                  pl.BlockSpec((1, N_HEADS), lambda b, i: (0, 0))],
        out_specs=pl.BlockSpec((tq, D_MODEL), qmap), out_shape=jax.ShapeDtypeStruct((batch * seq, D_MODEL), BF16),
        compiler_params=_params("parallel", "parallel"), name="swa_prompt_attn")(q, k, k, v, v, sinks)


def _expand_heads(x16):
    return jnp.concatenate([jnp.broadcast_to(x16[h:h + 1, :], (SUBLANES, LANES)) for h in range(N_HEADS)], axis=0)


def _decode_update(qbd, k, v, bias, mask, m_s, l_s, acc_s):
    s = _dot_nt(qbd, k.astype(BF16))
    if bias is not None:
        s = s + bias
    if mask is not None:
        s = jnp.where(mask, s, NEG)
    m_old = m_s[...]
    m_new = jnp.maximum(m_old, jnp.max(s, axis=1, keepdims=True))
    alpha = jnp.exp(m_old - m_new)
    p = jnp.exp(s - m_new)
    l_s[...] = alpha * l_s[...] + jnp.sum(p, axis=1, keepdims=True)
    m_s[...] = m_new
    acc_s[...] = alpha * acc_s[...] + _dot(p.astype(BF16), v.astype(BF16))


def _suffix_sums(lf):
    lane = lax.broadcasted_iota(jnp.int32, lf.shape, 1)
    x = lf
    sh = 1
    while sh < LANES:
        x = x + jnp.where(lane < LANES - sh, pltpu.roll(x, LANES - sh, 1), 0.0)
        sh *= 2
    return x


def _fox_sample_kernel(pt_ref, q_ref, knew_ref, vnew_ref, lfnew_ref, *rest, pages_per_step):
    n = pages_per_step
    k_refs, v_refs, lf_refs = rest[:n], rest[n:2 * n], rest[2 * n:3 * n]
    o_ref, m_s, l_s, acc_s, carry_s, rowc_s = rest[3 * n:]
    j = pl.program_id(1)
    row_q = lax.broadcasted_iota(jnp.int32, (LANES, LANES), 0) % SUBLANES
    lane = lax.broadcasted_iota(jnp.int32, (LANES, LANES), 1)

    def update(kt, vt, bias, mask):
        s = _dot(q_ref[...], kt) + bias
        if mask is not None:
            s = jnp.where(mask, s, NEG)
        m_old = m_s[...]
        m_new = jnp.maximum(m_old, jnp.max(s, axis=1, keepdims=True))
        alpha = jnp.exp(m_old - m_new)
        p = jnp.exp(s - m_new)
        l_s[...] = alpha * l_s[...] + jnp.sum(p, axis=1, keepdims=True)
        m_s[...] = m_new
        acc_s[...] = alpha * acc_s[...] + _dot_nt(p.astype(BF16), vt)

    @pl.when(j == 0)
    def _():
        m_s[...] = jnp.full_like(m_s, NEG)
        l_s[...] = jnp.zeros_like(l_s)
        acc_s[...] = jnp.zeros_like(acc_s)
        lf = lfnew_ref[...]
        inc = _suffix_sums(lf)
        bias = _expand_heads(inc - lf)
        rowc = -jnp.sum(jnp.where(lane == row_q, bias, 0.0), axis=1, keepdims=True)
        rowc_s[...] = rowc
        carry_s[...] = inc[:, 0:1]
        update(knew_ref[...].astype(BF16), vnew_ref[...].astype(BF16), bias + rowc, lane <= row_q)

    carry = carry_s[...]
    biases = []
    for i in range(n):
        lf = lf_refs[i][...]
        inc = _suffix_sums(lf)
        biases.append(_expand_heads(inc - lf + carry))
        carry = carry + inc[:, 0:1]
    carry_s[...] = carry
    kt = jnp.concatenate([r[...].astype(BF16) for r in k_refs], axis=1)
    vt = jnp.concatenate([r[...].astype(BF16) for r in v_refs], axis=1)
    update(kt, vt, jnp.concatenate(biases, axis=1) + rowc_s[...], None)

    @pl.when(j == pl.num_programs(1) - 1)
    def _():
        o_ref[...] = (acc_s[...] * (1.0 / l_s[...])).astype(BF16)


def _fox_sample(page_table, qbd, knew_t, vnew_t, lfnew, cache_kt, cache_vt, cache_lf_t, pages_per_step=32):
    db, n_pages = page_table.shape
    n = pages_per_step
    seq = lambda b, j, pt: (b, 0, 0)

    def page(i):
        return lambda b, j, pt: (pt[b, n_pages - 1 - (n * j + i)], 0, 0)

    in_specs = [pl.BlockSpec((None, LANES, D_KV), seq), pl.BlockSpec((None, D_KV, PAGE), seq),
                pl.BlockSpec((None, D_KV, PAGE), seq), pl.BlockSpec((None, N_HEADS, PAGE), seq)]
    in_specs += [pl.BlockSpec((None, D_KV, PAGE), page(i)) for i in range(n)]
    in_specs += [pl.BlockSpec((None, D_KV, PAGE), page(i)) for i in range(n)]
    in_specs += [pl.BlockSpec((None, N_HEADS, PAGE), page(i)) for i in range(n)]
    grid_spec = pltpu.PrefetchScalarGridSpec(
        num_scalar_prefetch=1, grid=(db, n_pages // n), in_specs=in_specs,
        out_specs=pl.BlockSpec((None, LANES, D_KV), seq),
        scratch_shapes=[pltpu.VMEM((LANES, 1), F32), pltpu.VMEM((LANES, 1), F32), pltpu.VMEM((LANES, D_KV), F32),
                        pltpu.VMEM((N_HEADS, 1), F32), pltpu.VMEM((LANES, 1), F32)])
    return pl.pallas_call(
        functools.partial(_fox_sample_kernel, pages_per_step=n), grid_spec=grid_spec,
        out_shape=jax.ShapeDtypeStruct((db, LANES, D_KV), BF16),
        compiler_params=_params("parallel", "arbitrary"), name="fox_sample_attn")(
            page_table, qbd, knew_t, vnew_t, lfnew, *([cache_kt] * n), *([cache_vt] * n), *([cache_lf_t] * n))


def _swa_sample_kernel(qbd_ref, kbuf_ref, vbuf_ref, knew_ref, vnew_ref, sink_ref, o_ref, m_s, l_s, acc_s):
    qbd = qbd_ref[...]
    row_q = lax.broadcasted_iota(jnp.int32, (LANES, LANES), 0) % SUBLANES
    lane = lax.broadcasted_iota(jnp.int32, (LANES, LANES), 1)
    m_s[...] = jnp.full_like(m_s, NEG)
    l_s[...] = jnp.zeros_like(l_s)
    acc_s[...] = jnp.zeros_like(acc_s)
    _decode_update(qbd, kbuf_ref[...], vbuf_ref[...], None, lane > row_q, m_s, l_s, acc_s)
    _decode_update(qbd, knew_ref[...], vnew_ref[...], None, lane <= row_q, m_s, l_s, acc_s)
    l = l_s[...] + jnp.exp(sink_ref[...] - m_s[...])
    o_ref[...] = (acc_s[...] * (1.0 / l)).astype(BF16)


def _swa_sample(qbd, kbuf, vbuf, knew, vnew, sink_rows):
    db = qbd.shape[0]
    seq = lambda b: (b, 0, 0)
    blk = pl.BlockSpec((None, LANES, D_KV), seq)
    return pl.pallas_call(
        _swa_sample_kernel, grid=(db,),
        in_specs=[blk, blk, blk, blk, blk, pl.BlockSpec((LANES, 1), lambda b: (0, 0))],
        out_specs=blk, out_shape=jax.ShapeDtypeStruct((db, LANES, D_KV), BF16),
        scratch_shapes=[pltpu.VMEM((LANES, 1), F32), pltpu.VMEM((LANES, 1), F32), pltpu.VMEM((LANES, D_KV), F32)],
        compiler_params=_params("parallel"), name="swa_sample_attn")(qbd, kbuf, vbuf, knew, vnew, sink_rows)


def _norm_kernel(x_ref, g_ref, h_ref):
    h_ref[...] = _rms(x_ref[...], g_ref[...])


def _norm(x, g, tm=512):
    t = x.shape[0]
    row = lambda i: (i, 0)
    return pl.pallas_call(
        _norm_kernel, grid=(t // tm,),
        in_specs=[pl.BlockSpec((tm, D_MODEL), row), pl.BlockSpec((1, D_MODEL), lambda i: (0, 0))],
        out_specs=pl.BlockSpec((tm, D_MODEL), row), out_shape=jax.ShapeDtypeStruct((t, D_MODEL), F32),
        compiler_params=_params("parallel"), name="pre_norm")(x, g)


def _rwkv_proj_kernel(x_ref, xprev_ref, sexp_ref, gpre_ref, mu_ref, wr_ref, wk_ref, wv_ref, w1_ref, w2_ref, a1_ref, a2_ref,
                      g1_ref, g2_ref, vec_ref, gmat_ref, r_ref, w_ref, k_ref, v_ref, kk_ref, kka_ref, g_ref, *,
                      tm, prompt_tiles, seq_p, seq_s):
    i = pl.program_id(0)
    gpre = gpre_ref[...]
    h = _rms(x_ref[...], gpre)
    before = _rms(xprev_ref[...], gpre)[SUBLANES - 1:SUBLANES, :]
    ridx = lax.broadcasted_iota(jnp.int32, (tm, 1), 0)
    hp = jnp.where(ridx == 0, before, pltpu.roll(h, 1, 0))
    starts_prompt_seq = (i * tm) % seq_p == 0
    hp_prompt = jnp.where((ridx == 0) & starts_prompt_seq, 0.0, hp)
    hp_sample = jnp.where(ridx % seq_s == 0, sexp_ref[...], hp)
    hp = jnp.where(i < prompt_tiles, hp_prompt, hp_sample)
    xx = hp - h

    def mix(i):
        return (h + xx * mu_ref[i:i + 1, :]).astype(BF16)

    w0, a0, k_k, k_a = (vec_ref[i:i + 1, :] for i in range(4))
    r = _dot(mix(0), wr_ref[...])
    k = _dot(mix(2), wk_ref[...])
    v = _dot(mix(3), wv_ref[...])
    lw = _dot(jnp.tanh(_dot(mix(1), w1_ref[...])).astype(BF16), w2_ref[...])
    w_log = -_softplus(-(w0 + lw)) - 0.5
    decay = jnp.exp(-jnp.exp(w_log))
    a = jax.nn.sigmoid(a0 + _dot(_dot(mix(4), a1_ref[...]).astype(BF16), a2_ref[...]))
    g = _dot(jax.nn.sigmoid(_dot(mix(5), g1_ref[...])).astype(BF16), g2_ref[...])
    kk = k * k_k
    ss = _split_dot(_tile_sum(kk * kk), gmat_ref[...], 3)
    kk = kk * _tile8(lax.rsqrt(jnp.maximum(ss, 1e-24)))
    r_ref[...] = r
    w_ref[...] = decay
    k_ref[...] = k * (1.0 + (a - 1.0) * k_a)
    v_ref[...] = v
    kk_ref[...] = kk
    kka_ref[...] = kk * a
    g_ref[...] = g


def _rwkv_proj(x, shift_rows, gpre, mu, wr, wk, wv, w1, w2, a1, a2, g1, g2, vecs, gmat, tp, seq_p, seq_s, tm=256):
    t = x.shape[0]
    n_p = tp // tm
    row = lambda i: (i, 0)
    fixed = lambda i: (0, 0)
    full = lambda a: pl.BlockSpec(a.shape, fixed)
    tok = pl.BlockSpec((tm, D_MODEL), row)
    prev = pl.BlockSpec((SUBLANES, D_MODEL), lambda i: (jnp.maximum(i * (tm // SUBLANES) - 1, 0), 0))
    carried = pl.BlockSpec((tm, D_MODEL), lambda i: (jnp.maximum(i - n_p, 0), 0))
    consts = [gpre, mu, wr, wk, wv, w1, w2, a1, a2, g1, g2, vecs, gmat]
    return pl.pallas_call(
        functools.partial(_rwkv_proj_kernel, tm=tm, prompt_tiles=n_p, seq_p=seq_p, seq_s=seq_s), grid=(t // tm,),
        in_specs=[tok, prev, carried] + [full(a) for a in consts],
        out_specs=[tok] * 7, out_shape=[jax.ShapeDtypeStruct((t, D_MODEL), F32)] * 7,
        compiler_params=_params("parallel"), name="rwkv_proj")(x, x, shift_rows, *consts)


def _rwkv_scan_kernel(*refs, nb, tc):
    ins = refs[:6 * nb]
    r_refs, w_refs, k_refs, v_refs, kk_refs, kka_refs = (ins[i * nb:(i + 1) * nb] for i in range(6))
    s0_ref, gmat_ref, y_ref, sout_ref, s_s = refs[6 * nb:]
    tt = pl.program_id(1)
    groups = HEAD_DIM // SUBLANES
    eye = (lax.broadcasted_iota(jnp.int32, (SUBLANES, LANES), 1) // N_HEADS
           == lax.broadcasted_iota(jnp.int32, (SUBLANES, LANES), 0))
    gmat = gmat_ref[...]

    @pl.when(tt == 0)
    def _():
        s_s[...] = s0_ref[...]

    def bcast(ref, t):
        return jnp.broadcast_to(ref[pl.ds(t, 1), :], (SUBLANES, D_MODEL))

    def step(t, _):
        kk = [bcast(kk_refs[b], t) for b in range(nb)]
        sa_parts = []
        v_parts = []
        for b in range(nb):
            vrow = bcast(v_refs[b], t)
            for g in range(groups):
                sa_parts.append(_tile_sum(s_s[b, g * SUBLANES:(g + 1) * SUBLANES, :] * kk[b]))
                v_parts.append(jnp.where(eye, vrow[:, g * LANES:(g + 1) * LANES], 0.0))
        sa_all = _split_dot(jnp.concatenate(sa_parts, axis=0), gmat, 2)
        v_all = _split_dot(jnp.concatenate(v_parts, axis=0), gmat, 2)
        y_parts = []
        for b in range(nb):
            w = bcast(w_refs[b], t)
            kka = bcast(kka_refs[b], t)
            k = bcast(k_refs[b], t)
            r = bcast(r_refs[b], t)
            for g in range(groups):
                lo = (b * groups + g) * SUBLANES
                sa = _tile8(-sa_all[lo:lo + SUBLANES, :])
                vv = _tile8(v_all[lo:lo + SUBLANES, :])
                s_new = s_s[b, g * SUBLANES:(g + 1) * SUBLANES, :] * w + sa * kka + vv * k
                s_s[b, g * SUBLANES:(g + 1) * SUBLANES, :] = s_new
                y_parts.append(_tile_sum(s_new * r))
        y_all = _split_dot(jnp.concatenate(y_parts, axis=0), gmat, 2)
        t8 = pl.multiple_of((t // SUBLANES) * SUBLANES, SUBLANES)
        mine = lax.broadcasted_iota(jnp.int32, (SUBLANES, LANES), 0) == t % SUBLANES
        for b in range(nb):
            for g in range(groups):
                lo = (b * groups + g) * SUBLANES
                yrow = jnp.sum(jnp.where(eye, y_all[lo:lo + SUBLANES, :], 0.0), axis=0, keepdims=True)
                cur = y_ref[b, pl.ds(t8, SUBLANES), g * LANES:(g + 1) * LANES]
                y_ref[b, pl.ds(t8, SUBLANES), g * LANES:(g + 1) * LANES] = jnp.where(
                    mine, jnp.broadcast_to(yrow, (SUBLANES, LANES)), cur)
        return 0

    y_ref[...] = jnp.zeros_like(y_ref)
    lax.fori_loop(0, tc, step, 0)

    @pl.when(tt == pl.num_programs(1) - 1)
    def _():
        sout_ref[...] = s_s[...]


def _rwkv_scan(vecs, s0, gmat, batch, seq, row0, nb, tc):
    nt = seq // tc
    blk0 = row0 // tc

    def tok(n):
        return pl.BlockSpec((tc, D_MODEL), lambda bb, tt: (blk0 + (bb * nb + n) * nt + tt, 0))

    in_specs = [tok(n) for _ in range(6) for n in range(nb)]
    args = [a for a in vecs for _ in range(nb)]
    state = pl.BlockSpec((nb, HEAD_DIM, D_MODEL), lambda bb, tt: (bb, 0, 0))
    in_specs += [state, pl.BlockSpec((LANES, LANES), lambda bb, tt: (0, 0))]
    return pl.pallas_call(
        functools.partial(_rwkv_scan_kernel, nb=nb, tc=tc), grid=(batch // nb, nt), in_specs=in_specs,
        out_specs=[pl.BlockSpec((nb, tc, D_MODEL), lambda bb, tt: (bb, tt, 0)), state],
        out_shape=[jax.ShapeDtypeStruct((batch, seq, D_MODEL), F32),
                   jax.ShapeDtypeStruct((batch, HEAD_DIM, D_MODEL), F32)],
        scratch_shapes=[pltpu.VMEM((nb, HEAD_DIM, D_MODEL), F32)],
        compiler_params=_params("parallel", "arbitrary"), name="rwkv_scan")(*args, s0, gmat)


def _rwkv_out_kernel(yp_ref, ys_ref, r_ref, k_ref, v_ref, g_ref, x_ref, vec_ref, gmat_ref, wo_ref, gpost_ref, o_ref, *,
                     prompt_tiles):
    r_k, ln_w, ln_b = (vec_ref[i:i + 1, :] for i in range(3))
    gmat = gmat_ref[...]
    y = jnp.where(pl.program_id(0) < prompt_tiles, yp_ref[...], ys_ref[...])
    mean = _split_dot(_tile_sum(y), gmat, 3) * (1.0 / HEAD_DIM)
    yc = y - _tile8(mean)
    var = _split_dot(_tile_sum(yc * yc), gmat, 3) * (1.0 / HEAD_DIM)
    yn = yc * _tile8(lax.rsqrt(var + GN_EPS)) * ln_w + ln_b
    bonus = _split_dot(_tile_sum(r_ref[...] * k_ref[...] * r_k), gmat, 3)
    out = (yn + _tile8(bonus) * v_ref[...]) * g_ref[...]
    m = _dot(out.astype(BF16), wo_ref[...])
    o_ref[...] = x_ref[...] + _rms(m, gpost_ref[...])


def _rwkv_out(y_p, y_s, r, k, v, g, x, vecs, gmat, wo, gpost, tm=256):
    t = x.shape[0]
    n_p = y_p.shape[0] // tm
    row = lambda i: (i, 0)
    fixed = lambda i: (0, 0)
    tok = pl.BlockSpec((tm, D_MODEL), row)
    consts = [vecs, gmat, wo, gpost]
    y_specs = [pl.BlockSpec((tm, D_MODEL), lambda i: (jnp.minimum(i, n_p - 1), 0)),
               pl.BlockSpec((tm, D_MODEL), lambda i: (jnp.maximum(i - n_p, 0), 0))]
    return pl.pallas_call(
        functools.partial(_rwkv_out_kernel, prompt_tiles=n_p), grid=(t // tm,),
        in_specs=y_specs + [tok] * 5 + [pl.BlockSpec(a.shape, fixed) for a in consts],
        out_specs=tok, out_shape=jax.ShapeDtypeStruct((t, D_MODEL), F32),
        compiler_params=_params("parallel"), name="rwkv_out")(y_p, y_s, r, k, v, g, x, *consts)


def _block_diag_q(q_perm, db, ds):
    inv = np.argsort(np.asarray(HEAD_PERM))
    q = q_perm.reshape(db, ds, N_HEADS, HEAD_DIM)[:, :, inv]
    q = q.reshape(db, ds, KV_HEADS, N_HEADS // KV_HEADS, HEAD_DIM).transpose(0, 2, 3, 1, 4)
    eye = jnp.eye(KV_HEADS, dtype=q.dtype)
    qbd = q[:, :, :, :, None, :] * eye[None, :, None, None, :, None]
    return qbd.reshape(db, N_HEADS * ds, D_KV)


def _undo_block_diag(o, db, ds):
    o = o.reshape(db, KV_HEADS, N_HEADS // KV_HEADS, ds, KV_HEADS, HEAD_DIM)
    o = jnp.stack([o[:, kv, :, :, kv, :] for kv in range(KV_HEADS)], axis=1)
    o = o.transpose(0, 3, 1, 2, 4).reshape(db, ds, N_HEADS, HEAD_DIM)
    return o[:, :, np.asarray(HEAD_PERM)].reshape(db * ds, D_MODEL)


def _pad_page(x, db, ds):
    return jnp.pad(x.reshape(db, ds, -1), ((0, 0), (0, PAGE - ds), (0, 0)))


def _rope_tables(pos):
    inv_freq = ROPE_THETA ** (-jnp.arange(0, ROT_DIM, 2, dtype=F32) / ROT_DIM)
    ang = pos.astype(F32)[:, None] * inv_freq[None, :]
    cos, sin = jnp.cos(ang), jnp.sin(ang)
    half = ROT_DIM // 2
    ones = jnp.ones((pos.shape[0], HEAD_DIM - ROT_DIM), F32)
    zeros = jnp.zeros((pos.shape[0], HEAD_DIM - ROT_DIM), F32)
    zh = jnp.zeros_like(sin)
    a = jnp.concatenate([cos, cos, ones], axis=1)
    m = jnp.concatenate([-sin, zh, zeros], axis=1)
    p = jnp.concatenate([zh, sin, zeros], axis=1)
    rep = LANES // HEAD_DIM
    return tuple(jnp.tile(z, (1, rep)) for z in (a, m, p))


def kernel(x_prompt, x_sample, cache_fox_k_l0, cache_fox_v_l0, cache_fox_logf_l0, state_rwkv_wkv_l1, state_rwkv_shift_l1, cache_swa_k_l2, cache_swa_v_l2, cache_fox_k_l3, cache_fox_v_l3, cache_fox_logf_l3, page_table, p_prompt, p_sample, norm_mix_pre, norm_mix_post, norm_ffn_pre, norm_ffn_post, ffn_w_up, ffn_w_down, ple_w_proj, ple_w_gate, fox_w_q, fox_w_k, fox_w_v, fox_w_f, fox_b_f, fox_w_o, rwkv_mu, rwkv_w_r, rwkv_w_k, rwkv_w_v, rwkv_w_o, rwkv_w0, rwkv_w1, rwkv_w2, rwkv_a0, rwkv_a1, rwkv_a2, rwkv_g1, rwkv_g2, rwkv_k_k, rwkv_k_a, rwkv_r_k, rwkv_ln_w, rwkv_ln_b, swa_w_q, swa_b_q, swa_w_k, swa_b_k, swa_w_v, swa_b_v, swa_sinks, swa_w_o):
    bp, sp, _ = x_prompt.shape
    db, ds, _ = x_sample.shape
    depth = norm_mix_pre.shape[0]
    tp, ts = bp * sp, db * ds
    n_pages = page_table.shape[1]
    past = n_pages * PAGE
    bf = lambda a: a.astype(BF16)
    vec = lambda a: a.reshape(1, -1).astype(F32)

    x = jnp.concatenate([x_prompt.reshape(tp, D_MODEL), x_sample.reshape(ts, D_MODEL)], axis=0)
    pp_all = p_prompt.reshape(depth * tp, D_PLE)
    ps_all = p_sample.reshape(depth * ts, D_PLE)
    pos = jnp.concatenate([jnp.tile(jnp.arange(sp), bp), jnp.tile(past + jnp.arange(ds), db)])
    gmat = (np.arange(LANES)[:, None] % N_HEADS == np.arange(LANES)[None, :] % N_HEADS)
    gmat = jnp.asarray(gmat, BF16)
    fox_caches = ((cache_fox_k_l0, cache_fox_v_l0, cache_fox_logf_l0), (cache_fox_k_l3, cache_fox_v_l3, cache_fox_logf_l3))
    new_state = []

    for i in range(depth):
        kind, j = i % N_MIXERS, i // N_MIXERS
        g_pre = vec(norm_mix_pre[i])
        g_post = vec(norm_mix_post[i])
        if kind == 0:
            wf = jnp.pad(fox_w_f[j], ((0, 0), (0, LANES - N_HEADS)))
            w = bf(jnp.concatenate([fox_w_q[j][:, Q_COLS], fox_w_k[j], fox_w_v[j], wf], axis=1))
            b = jnp.concatenate([jnp.zeros((D_MODEL + 2 * D_KV,), F32), fox_b_f[j], jnp.zeros((LANES - N_HEADS,), F32)])
            q, k, v, lf = _proj(x, g_pre, w, vec(b), None)
            lf = lf[:, :N_HEADS]
            lf_p = lf[:tp].reshape(bp, sp, N_HEADS)
            ck = _cumsum(lf_p.transpose(0, 2, 1))
            cq = ck.transpose(0, 2, 1).reshape(tp, N_HEADS)
            o_p = _flash(q, k, v, cq, bp, sp)
            ck_cache, cv_cache, clf_cache = fox_caches[j]
            n_pool = ck_cache.shape[0]
            pages_t = lambda c: c.transpose(0, 2, 3, 1).reshape(n_pool, D_KV, PAGE)
            new_t = lambda z: _pad_page(z, db, ds).transpose(0, 2, 1)
            o_s = _fox_sample(
                page_table, _block_diag_q(q[tp:], db, ds), new_t(k[tp:]), new_t(v[tp:]), new_t(lf[tp:]),
                pages_t(ck_cache), pages_t(cv_cache), clf_cache.transpose(0, 2, 1))
            x = _oproj(o_p, _undo_block_diag(o_s, db, ds), bf(fox_w_o[j][Q_COLS, :]), g_post, x)
            new_state.append((k[:tp].reshape(bp, sp, KV_HEADS, HEAD_DIM), v[:tp].reshape(bp, sp, KV_HEADS, HEAD_DIM),
                              lf_p, k[tp:].reshape(db, ds, KV_HEADS, HEAD_DIM),
                              v[tp:].reshape(db, ds, KV_HEADS, HEAD_DIM), lf[tp:].reshape(db, ds, N_HEADS)))
        elif kind == 1:
            last = jnp.concatenate([x[:tp].reshape(bp, sp, D_MODEL)[:, -1], x[tp:].reshape(db, ds, D_MODEL)[:, -1]])
            n_last = -(-(bp + db) // SUBLANES) * SUBLANES
            h_last = _norm(jnp.pad(last, ((0, n_last - bp - db), (0, 0))), g_pre, tm=n_last)
            shift_rows = jnp.repeat(state_rwkv_shift_l1.astype(F32), ds, axis=0)
            pc = lambda a: a[:, R_COLS]
            vecs = jnp.stack([rwkv_w0[j][R_COLS], rwkv_a0[j][R_COLS], rwkv_k_k[j][R_COLS], rwkv_k_a[j][R_COLS]])
            r, w, k, v, kk, kka, g = _rwkv_proj(
                x, shift_rows, g_pre, rwkv_mu[j], bf(pc(rwkv_w_r[j])), bf(pc(rwkv_w_k[j])), bf(pc(rwkv_w_v[j])),
                bf(rwkv_w1[j]), bf(pc(rwkv_w2[j])), bf(rwkv_a1[j]), bf(pc(rwkv_a2[j])), bf(rwkv_g1[j]),
                bf(pc(rwkv_g2[j])), vecs, gmat, tp, sp, ds)
            scan_in = (r, w, k, v, kk, kka)
            to_lanes = lambda s: s.transpose(0, 2, 3, 1).reshape(s.shape[0], HEAD_DIM, D_MODEL)
            from_lanes = lambda s: s.reshape(s.shape[0], HEAD_DIM, HEAD_DIM, N_HEADS).transpose(0, 3, 1, 2)
            y_p, s_p = _rwkv_scan(scan_in, jnp.zeros((bp, HEAD_DIM, D_MODEL), F32), gmat, bp, sp, 0, bp, 64)
            y_s, s_s = _rwkv_scan(scan_in, to_lanes(state_rwkv_wkv_l1.astype(F32)), gmat, db, ds, tp, 4, ds)
            ovecs = jnp.stack([rwkv_r_k[j].reshape(-1)[R_COLS], rwkv_ln_w[j][R_COLS], rwkv_ln_b[j][R_COLS]])
            x = _rwkv_out(y_p.reshape(tp, D_MODEL), y_s.reshape(ts, D_MODEL), r, k, v, g, x, ovecs, gmat,
                          bf(rwkv_w_o[j][R_COLS, :]), g_post)
            new_state.append((from_lanes(s_p), h_last[:bp], from_lanes(s_s), h_last[bp:bp + db]))
        else:
            w = bf(jnp.concatenate([swa_w_q[j][:, Q_COLS], swa_w_k[j], swa_w_v[j]], axis=1))
            b = jnp.concatenate([swa_b_q[j][Q_COLS], swa_b_k[j], swa_b_v[j]])
            q, k, v = _proj(x, g_pre, w, vec(b), _rope_tables(pos))
            o_p = _swa_band(q, k, v, vec(swa_sinks[j]), bp, sp)
            keep = cache_swa_k_l2.shape[1]
            k_s = k[tp:].reshape(db, ds, D_KV)
            v_s = v[tp:].reshape(db, ds, D_KV)
            kbuf = cache_swa_k_l2.reshape(db, keep, D_KV)
            vbuf = cache_swa_v_l2.reshape(db, keep, D_KV)
            sink_rows = jnp.repeat(swa_sinks[j].astype(F32), ds).reshape(N_HEADS * ds, 1)
            o_s = _swa_sample(_block_diag_q(q[tp:], db, ds), kbuf, vbuf, _pad_page(k[tp:], db, ds),
                              _pad_page(v[tp:], db, ds), sink_rows)
            x = _oproj(o_p, _undo_block_diag(o_s, db, ds), bf(swa_w_o[j][Q_COLS, :]), g_post, x)
            k_p = k[:tp].reshape(bp, sp, KV_HEADS, HEAD_DIM)
            v_p = v[:tp].reshape(bp, sp, KV_HEADS, HEAD_DIM)
            wk = min(WINDOW, sp)
            new_state.append((k_p[:, sp - wk:], v_p[:, sp - wk:],
                              jnp.concatenate([kbuf, k_s], axis=1)[:, ds:].reshape(db, keep, KV_HEADS, HEAD_DIM),
                              jnp.concatenate([vbuf, v_s], axis=1)[:, ds:].reshape(db, keep, KV_HEADS, HEAD_DIM)))
        x = _ffn(x, vec(norm_ffn_pre[i]), bf(ffn_w_up[i]), bf(ffn_w_down[i]), vec(norm_ffn_post[i]), pp_all, ps_all, i,
                 tp, bf(ple_w_proj[i]), bf(ple_w_gate[i]))

    outs = [x[:tp].reshape(bp, sp, D_MODEL), x[tp:].reshape(db, ds, D_MODEL)]
    for st in new_state:
        outs.extend(st)
    return tuple(outs)
```
